```python
import jax, jax.numpy as jnp
from jax import lax
import numpy as np

D_MODEL = 1024
BATCH = 8
SEQ = 4096
DEPTH = 1
DEC_BATCH = 8
DEC_SEQ = 16
PAST_LEN = 2048

CHUNK = 64
EPS = 1e-6
D_INNER = 2 * D_MODEL
SSD_HEAD_DIM = 64
SSD_HEADS = D_INNER // SSD_HEAD_DIM
SSD_GROUPS = 8
SSD_HPG = SSD_HEADS // SSD_GROUPS
SSD_STATE = 128
CONV_WIDTH = 4
CONV_DIM = D_INNER + 2 * SSD_GROUPS * SSD_STATE
SSD_NORM_GROUP = D_INNER // SSD_GROUPS
POOL_WINDOWS = (2, 4, 8, 16)
POOL_GROUPS = len(POOL_WINDOWS)
D_POOL = D_MODEL
POOL_GROUP_DIM = D_POOL // POOL_GROUPS
POOL_HIST = max(POOL_WINDOWS) - 1
D_FF = 4 * D_MODEL
N_BRANCH = 2
SPLIT_Z = D_INNER
SPLIT_XBC = SPLIT_Z + CONV_DIM
SPLIT_DT = SPLIT_XBC + SSD_HEADS
SPLIT_POOL = SPLIT_DT + D_POOL
D_IN_PROJ = SPLIT_POOL + N_BRANCH * D_MODEL

kernel_name = 'hybrid_ssd_pool_streaming_encoder_step'


def rms_norm(x, g):
    xf = x.astype(jnp.float32)
    y = xf * lax.rsqrt(jnp.mean(xf * xf, axis=-1, keepdims=True) + EPS)
    return (y * g.astype(jnp.float32)).astype(x.dtype)


def causal_dwconv(hist, u, w, b):
    L = u.shape[1]
    up = jnp.concatenate([hist, u], axis=1)
    out = b + up[:, 0:L] * w[0]
    for k in range(1, CONV_WIDTH):
        out = out + up[:, k:k + L] * w[k]
    return out, up[:, -(CONV_WIDTH - 1):]


def ssd_chunked(x, dt, a, bm, cm, h0):
    f32 = jnp.float32
    b, L = x.shape[:2]
    cl = CHUNK if L % CHUNK == 0 else L
    nc = L // cl
    G, J, P, N = SSD_GROUPS, SSD_HPG, SSD_HEAD_DIM, SSD_STATE
    xr = (x.astype(f32) * dt[..., None]).reshape(b, nc, cl, G, J, P)
    a_cs = jnp.cumsum((dt * a).reshape(b, nc, cl, G, J), axis=2)
    br = bm.astype(f32).reshape(b, nc, cl, G, N)
    cr = cm.astype(f32).reshape(b, nc, cl, G, N)
    seg = a_cs[:, :, :, None] - a_cs[:, :, None, :]
    causal = jnp.tril(jnp.ones((cl, cl), dtype=bool))[None, None, :, :, None, None]
    decay = jnp.exp(jnp.where(causal, seg, -jnp.inf))
    cb = jnp.einsum('bclgn,bcsgn->bclsg', cr, br)
    y_diag = jnp.einsum('bclsg,bclsgj,bcsgjp->bclgjp', cb, decay, xr)
    decay_to_end = jnp.exp(a_cs[:, :, -1:] - a_cs)
    states = jnp.einsum('bclgn,bclgj,bclgjp->bcgjpn', br, decay_to_end, xr)
    chunk_decay = jnp.exp(a_cs[:, :, -1])

    def step(h, inp):
        s_c, d_c = inp
        return h * d_c[..., None, None] + s_c, h

    h_final, h_prev = lax.scan(step, h0.astype(f32),
                               (jnp.moveaxis(states, 1, 0), jnp.moveaxis(chunk_decay, 1, 0)))
    h_prev = jnp.moveaxis(h_prev, 0, 1)
    y_off = jnp.einsum('bclgn,bcgjpn,bclgj->bclgjp', cr, h_prev, jnp.exp(a_cs))
    y = (y_diag + y_off).reshape(b, L, G, J, P)
    return y, h_final


def multiscale_pool(hist, p, pos0):
    f32 = jnp.float32
    b, L, _ = p.shape
    pp = jnp.concatenate([hist, p], axis=1).astype(f32)
    cs = jnp.concatenate([jnp.zeros_like(pp[:, :1]), jnp.cumsum(pp, axis=1)], axis=1)
    pos = pos0 + jnp.arange(L)
    outs = []
    for gi, w in enumerate(POOL_WINDOWS):
        sl = slice(gi * POOL_GROUP_DIM, (gi + 1) * POOL_GROUP_DIM)
        hi = cs[:, POOL_HIST + 1:POOL_HIST + 1 + L, sl]
        lo = cs[:, POOL_HIST + 1 - w:POOL_HIST + 1 - w + L, sl]
        cnt = jnp.minimum(pos + 1, w).astype(f32)[None, :, None]
        outs.append((hi - lo) / cnt)
    mean = jnp.concatenate(outs, axis=-1)
    return (mean - p.astype(f32)).astype(p.dtype), pp[:, -POOL_HIST:].astype(p.dtype)


def encoder_layer(x, c, conv_hist, pool_hist, h0, pos0, w_ada, b_ada, g_pre_mix, g_post_mix,
                  g_pre_mlp, g_post_mlp, w_in, conv_w, conv_b, dt_bias, a_log, d_skip, g_ssd_norm,
                  w_ssd_out, w_pool_group, pool_scale, w_o, w_up, w_down):
    f32 = jnp.float32
    b, L, _ = x.shape
    mod = (jax.nn.silu(c) @ w_ada + b_ada)[:, None, :]
    sh1, sc1, gt1, sh2, sc2, gt2 = jnp.split(mod, 6, axis=-1)
    u = rms_norm(x, g_pre_mix) * (1 + sc1) + sh1
    proj = u @ w_in
    z, xbc, dt_raw, p, gates = jnp.split(proj, [SPLIT_Z, SPLIT_XBC, SPLIT_DT, SPLIT_POOL], axis=-1)
    xbc_c, conv_state = causal_dwconv(conv_hist, xbc, conv_w, conv_b)
    xbc_c = jax.nn.silu(xbc_c)
    xs, bm, cm = jnp.split(xbc_c, [D_INNER, D_INNER + SSD_GROUPS * SSD_STATE], axis=-1)
    xs = xs.reshape(b, L, SSD_GROUPS, SSD_HPG, SSD_HEAD_DIM)
    bm = bm.reshape(b, L, SSD_GROUPS, SSD_STATE)
    cm = cm.reshape(b, L, SSD_GROUPS, SSD_STATE)
    dt = jax.nn.softplus(dt_raw.astype(f32) + dt_bias.astype(f32)).reshape(b, L, SSD_GROUPS, SSD_HPG)
    a = -jnp.exp(a_log.astype(f32)).reshape(SSD_GROUPS, SSD_HPG)
    h0g = h0.reshape(b, SSD_GROUPS, SSD_HPG, SSD_HEAD_DIM, SSD_STATE)
    y, h_final = ssd_chunked(xs, dt, a, bm, cm, h0g)
    y = y + xs.astype(f32) * d_skip.astype(f32).reshape(SSD_GROUPS, SSD_HPG)[..., None]
    yg = (y.reshape(b, L, D_INNER) * jax.nn.silu(z.astype(f32))).reshape(b, L, SSD_GROUPS, SSD_NORM_GROUP)
    yg = yg * lax.rsqrt(jnp.mean(yg * yg, axis=-1, keepdims=True) + EPS)
    y_ssd = (yg.reshape(b, L, D_INNER) * g_ssd_norm.astype(f32)).astype(x.dtype) @ w_ssd_out
    pm, pool_state = multiscale_pool(pool_hist, p, pos0)
    y_pool = jnp.einsum('blgc,gcd->blgd', pm.reshape(b, L, POOL_GROUPS, POOL_GROUP_DIM),
                        w_pool_group).reshape(b, L, D_POOL) * pool_scale
    gate_ssd, gate_pool = jnp.split(jax.nn.sigmoid(gates.astype(f32)).astype(x.dtype), N_BRANCH, axis=-1)
    mix = (gate_ssd * y_ssd + gate_pool * y_pool) @ w_o
    x = x + gt1 * rms_norm(mix, g_post_mix)
    v = rms_norm(x, g_pre_mlp) * (1 + sc2) + sh2
    hdn = jnp.square(jax.nn.relu(v @ w_up))
    x = x + gt2 * rms_norm(hdn @ w_down, g_post_mlp)
    new_h = h_final.reshape(b, SSD_HEADS, SSD_HEAD_DIM, SSD_STATE).astype(x.dtype)
    return x, new_h, conv_state, pool_state


def setup_inputs(seed: int = 0) -> dict:
    key = jax.random.key(seed)
    ks = jax.random.split(key, 32)
    nrm = jax.random.normal
    f32 = jnp.float32
    dt0 = jnp.exp(jax.random.uniform(ks[13], (DEPTH, SSD_HEADS), f32, np.log(1e-3), np.log(1e-1)))
    return {
        'x_prompt': nrm(ks[0], (BATCH, SEQ, D_MODEL), f32),
        'x_sample': nrm(ks[1], (DEC_BATCH, DEC_SEQ, D_MODEL), f32),
        'state_ssm': 0.5 * nrm(ks[2], (DEPTH, DEC_BATCH, SSD_HEADS, SSD_HEAD_DIM, SSD_STATE), f32),
        'state_conv': nrm(ks[3], (DEPTH, DEC_BATCH, CONV_WIDTH - 1, CONV_DIM), f32),
        'state_pool': nrm(ks[4], (DEPTH, DEC_BATCH, POOL_HIST, D_POOL), f32),
        'c_prompt': nrm(ks[5], (BATCH, D_MODEL), f32),
        'c_sample': nrm(ks[6], (DEC_BATCH, D_MODEL), f32),
        'w_ada': 0.5 * D_MODEL ** -0.5 * nrm(ks[7], (DEPTH, D_MODEL, 6 * D_MODEL), f32),
        'b_ada': 0.01 * nrm(ks[8], (DEPTH, 6 * D_MODEL), f32),
        'g_pre_mix': 1.0 + 0.05 * nrm(ks[9], (DEPTH, D_MODEL), f32),
        'g_post_mix': 1.0 + 0.05 * nrm(ks[10], (DEPTH, D_MODEL), f32),
        'g_pre_mlp': 1.0 + 0.05 * nrm(ks[11], (DEPTH, D_MODEL), f32),
        'g_post_mlp': 1.0 + 0.05 * nrm(ks[12], (DEPTH, D_MODEL), f32),
        'w_in': D_MODEL ** -0.5 * nrm(ks[14], (DEPTH, D_MODEL, D_IN_PROJ), f32),
        'conv_w': 0.5 * nrm(ks[15], (DEPTH, CONV_WIDTH, CONV_DIM), f32),
        'conv_b': 0.01 * nrm(ks[16], (DEPTH, CONV_DIM), f32),
        'dt_bias': dt0 + jnp.log(-jnp.expm1(-dt0)),
        'a_log': jnp.log(jax.random.uniform(ks[17], (DEPTH, SSD_HEADS), f32, 1.0, 16.0)),
        'd_skip': 1.0 + 0.1 * nrm(ks[18], (DEPTH, SSD_HEADS), f32),
        'g_ssd_norm': 1.0 + 0.05 * nrm(ks[19], (DEPTH, D_INNER), f32),
        'w_ssd_out': D_INNER ** -0.5 * nrm(ks[20], (DEPTH, D_INNER, D_MODEL), f32),
        'w_pool_group': POOL_GROUP_DIM ** -0.5 * nrm(ks[21], (DEPTH, POOL_GROUPS, POOL_GROUP_DIM, POOL_GROUP_DIM), f32),
        'pool_scale': 1.0 + 0.1 * nrm(ks[22], (DEPTH, D_POOL), f32),
        'w_o': D_MODEL ** -0.5 * nrm(ks[23], (DEPTH, D_MODEL, D_MODEL), f32),
        'w_up': D_MODEL ** -0.5 * nrm(ks[24], (DEPTH, D_MODEL, D_FF), f32),
        'w_down': D_FF ** -0.5 * nrm(ks[25], (DEPTH, D_FF, D_MODEL), f32),
    }


def reference(x_prompt, x_sample, state_ssm, state_conv, state_pool, c_prompt, c_sample,
              w_ada, b_ada, g_pre_mix, g_post_mix, g_pre_mlp, g_post_mlp, w_in, conv_w, conv_b,
              dt_bias, a_log, d_skip, g_ssd_norm, w_ssd_out, w_pool_group, pool_scale, w_o, w_up, w_down):
    yp, ys = x_prompt, x_sample
    bp = x_prompt.shape[0]
    ssm_p, conv_p, pool_p, ssm_s, conv_s, pool_s = [], [], [], [], [], []
    for l in range(DEPTH):
        lw = dict(w_ada=w_ada[l], b_ada=b_ada[l], g_pre_mix=g_pre_mix[l], g_post_mix=g_post_mix[l],
                  g_pre_mlp=g_pre_mlp[l], g_post_mlp=g_post_mlp[l], w_in=w_in[l], conv_w=conv_w[l],
                  conv_b=conv_b[l], dt_bias=dt_bias[l], a_log=a_log[l], d_skip=d_skip[l],
                  g_ssd_norm=g_ssd_norm[l], w_ssd_out=w_ssd_out[l], w_pool_group=w_pool_group[l],
                  pool_scale=pool_scale[l], w_o=w_o[l], w_up=w_up[l], w_down=w_down[l])
        zc = jnp.zeros((bp, CONV_WIDTH - 1, CONV_DIM), yp.dtype)
        zp = jnp.zeros((bp, POOL_HIST, D_POOL), yp.dtype)
        zh = jnp.zeros((bp, SSD_HEADS, SSD_HEAD_DIM, SSD_STATE), yp.dtype)
        yp, h_p, c_p, p_p = encoder_layer(yp, c_prompt, zc, zp, zh, 0, **lw)
        ys, h_s, c_s, p_s = encoder_layer(ys, c_sample, state_conv[l], state_pool[l], state_ssm[l], PAST_LEN, **lw)
        ssm_p.append(h_p); conv_p.append(c_p); pool_p.append(p_p)
        ssm_s.append(h_s); conv_s.append(c_s); pool_s.append(p_s)
    return (yp, ys, jnp.stack(ssm_p), jnp.stack(conv_p), jnp.stack(pool_p),
            jnp.stack(ssm_s), jnp.stack(conv_s), jnp.stack(pool_s))
```

```python
import functools

import jax
import jax.numpy as jnp
from jax import lax
from jax.experimental import pallas as pl
from jax.experimental.pallas import tpu as pltpu

F32 = jnp.float32
BF16 = jnp.bfloat16

EPS = 1e-6
PAST_LEN = 2048
POOL_WINDOWS = (2, 4, 8, 16)
POOL_HIST = max(POOL_WINDOWS) - 1
CONV_WIDTH = 4
SSD_HEAD_DIM = 64
SSD_STATE = 128
LANES = 128
SSD_CHUNK = 128
GROUP_COLS = 256
HEADS_PER_GROUP = GROUP_COLS // SSD_HEAD_DIM
CONV_PAD = 8
POOL_PAD = 16
MIX_TILE = 256
MLP_TILE = 512
VMEM_LIMIT = 60 * 1024 * 1024


def _dot(a, b):
    return jnp.dot(a, b, preferred_element_type=F32)


def _dot_nt(a, b):
    return lax.dot_general(a, b, (((1,), (1,)), ((), ())), preferred_element_type=F32)


def _silu(v):
    return v * jax.nn.sigmoid(v)


def _rms(v, g):
    return v * lax.rsqrt(jnp.mean(v * v, axis=-1, keepdims=True) + EPS) * g


def _adaln_kernel(c_ref, w_ref, b_ref, o_ref):
    c = c_ref[...]
    o_ref[...] = jnp.dot(_silu(c), w_ref[...], preferred_element_type=F32,
                         precision=lax.Precision.HIGHEST) + b_ref[...]


def _adaln(c, w_ada, b_ada):
    rows, d = c.shape
    n = w_ada.shape[1]
    bn = d
    return pl.pallas_call(
        _adaln_kernel,
        out_shape=jax.ShapeDtypeStruct((rows, n), F32),
        grid=(n // bn,),
        in_specs=[pl.BlockSpec((rows, d), lambda j: (0, 0)),
                  pl.BlockSpec((d, bn), lambda j: (0, j)),
                  pl.BlockSpec((1, bn), lambda j: (0, j))],
        out_specs=pl.BlockSpec((rows, bn), lambda j: (0, j)),
        compiler_params=pltpu.CompilerParams(dimension_semantics=("arbitrary",)),
        name="adaln",
    )(c, w_ada, b_ada.reshape(1, n))


def _mix_kernel(x_ref, mod_ref, convh_ref, poolh_ref, h0_ref,
                gpre_ref, gpost_ref, wz_ref, wxbc_ref, wdt_ref, wp_ref, wg_ref,
                convw_ref, convb_ref, dtb_ref, alog_ref, dskip_ref, gssd_ref, e2_ref,
                wssd_ref, wpool_ref, pscale_ref, wo_ref,
                out_ref, hout_ref, convo_ref, poolo_ref,
                u_scr, xbc_scr, pp_scr, xr_scr, bm_scr, bmt_scr, cm_scr, acs_scr, acst_scr,
                y_scr, yn_scr, st_scr,
                *, lt, lv, pos0, d_inner, n_groups):
    q = SSD_CHUNK
    n_chunks = lt // q
    t = pl.program_id(1)
    last_t = pl.num_programs(1) - 1
    d_bc = n_groups * SSD_STATE
    slab = 512

    @pl.when(t == 0)
    def _init():
        xbc_scr[0:CONV_PAD, :] = convh_ref[0]
        pp_scr[0:POOL_PAD, :] = poolh_ref[0]
        for g in range(n_groups):
            cs = slice(g * GROUP_COLS, (g + 1) * GROUP_COLS)
            st_scr[:, cs] = h0_ref[0, cs, :].T

    x = x_ref[0]
    mod = mod_ref[0]
    u = _rms(x, gpre_ref[...]) * (1.0 + mod[1:2]) + mod[0:1]
    u_scr[...] = u.astype(BF16)

    for c0 in range(0, wxbc_ref.shape[1], slab):
        xbc_scr[CONV_PAD:CONV_PAD + lt, c0:c0 + slab] = _dot(u_scr[...], wxbc_ref[:, c0:c0 + slab])
    for c0 in range(0, wp_ref.shape[1], slab):
        pp_scr[POOL_PAD:POOL_PAD + lt, c0:c0 + slab] = _dot(u_scr[...], wp_ref[:, c0:c0 + slab])

    lane = lax.broadcasted_iota(jnp.int32, (1, LANES), 1)
    n_heads = d_inner // SSD_HEAD_DIM
    dt = jax.nn.softplus(_dot(u_scr[...], wdt_ref[...]) + dtb_ref[...])
    if lv < lt:
        row = lax.broadcasted_iota(jnp.int32, (lt, LANES), 0)
        dt = jnp.where(row < lv, dt, 0.0)
    a_row = jnp.where(lane < n_heads, -jnp.exp(alog_ref[...]), 0.0)
    da = dt * a_row
    ri = lax.broadcasted_iota(jnp.int32, (q, q), 0)
    ci = lax.broadcasted_iota(jnp.int32, (q, q), 1)
    causal = ri >= ci
    tri = causal.astype(F32)
    ea_parts, dte_parts = [], []
    for c in range(n_chunks):
        acs = jnp.dot(tri, da[c * q:(c + 1) * q], preferred_element_type=F32,
                      precision=lax.Precision.HIGHEST)
        acs_scr[c * q:(c + 1) * q, :] = acs
        acst_scr[c] = acs.T
        ea_parts.append(jnp.exp(acs))
        dte_parts.append(jnp.exp(acs[q - 1:q, :] - acs))

    def expand(v):
        hi = v.astype(BF16)
        lo = (v - hi.astype(F32)).astype(BF16)
        return jnp.concatenate([hi, lo], axis=1)

    dt2 = expand(dt)
    ea2 = expand(jnp.concatenate(ea_parts, axis=0))
    dte2 = expand(jnp.concatenate(dte_parts, axis=0))

    rows = pl.ds(CONV_PAD, lt)
    for c0 in range(0, d_inner + 2 * d_bc, slab):
        cs = slice(c0, c0 + slab)
        acc = convb_ref[:, cs] + xbc_scr[pl.ds(CONV_PAD - (CONV_WIDTH - 1), lt), cs] * convw_ref[0:1, cs]
        for k in range(1, CONV_WIDTH):
            acc = acc + xbc_scr[pl.ds(CONV_PAD - (CONV_WIDTH - 1) + k, lt), cs] * convw_ref[k:k + 1, cs]
        v = _silu(acc)
        if c0 < d_inner:
            y_scr[:, cs] = v * dskip_ref[:, cs]
            xr_scr[:, cs] = v * _dot(dt2, e2_ref[:, cs])
        elif c0 < d_inner + d_bc:
            b0 = c0 - d_inner
            bm_scr[:, b0:b0 + slab] = v.astype(BF16)
            for c in range(n_chunks):
                for gg in range(slab // SSD_STATE):
                    g = b0 // SSD_STATE + gg
                    bmt_scr[c * n_groups + g] = v[c * q:(c + 1) * q,
                                                  gg * SSD_STATE:(gg + 1) * SSD_STATE].T.astype(BF16)
        else:
            b0 = c0 - d_inner - d_bc
            cm_scr[:, b0:b0 + slab] = v.astype(BF16)

    @pl.when(t == last_t)
    def _conv_state():
        convo_ref[0] = xbc_scr[pl.ds(CONV_PAD + lv - (CONV_WIDTH - 1), CONV_WIDTH - 1), :]

    xbc_scr[0:CONV_PAD, :] = xbc_scr[lt:lt + CONV_PAD, :]
    for c0 in range(0, d_inner, slab):
        xbc_scr[rows, c0:c0 + slab] = _dot(ea2, e2_ref[:, c0:c0 + slab])
        xbc_scr[rows, d_inner + c0:d_inner + c0 + slab] = _dot(dte2, e2_ref[:, c0:c0 + slab])

    lane_g = lax.broadcasted_iota(jnp.int32, (1, GROUP_COLS), 1)
    head_masks = [((lane_g >= j * SSD_HEAD_DIM) & (lane_g < (j + 1) * SSD_HEAD_DIM)).astype(BF16)
                  for j in range(HEADS_PER_GROUP)]

    def chunk_body(c, carry):
        r0 = pl.multiple_of(c * q, q)
        crow = pl.ds(r0, q)
        erow = pl.ds(CONV_PAD + r0, q)
        acs = acs_scr[crow, :]
        acst = acst_scr[c]
        for g in range(n_groups):
            gs = slice(g * GROUP_COLS, (g + 1) * GROUP_COLS)
            ns = slice(g * SSD_STATE, (g + 1) * SSD_STATE)
            cg = cm_scr[crow, ns]
            cb = _dot_nt(cg, bm_scr[crow, ns])
            xr = xr_scr[crow, gs]
            xr_bf = xr.astype(BF16)
            ms, rs = [], []
            for j in range(HEADS_PER_GROUP):
                h = g * HEADS_PER_GROUP + j
                seg = acs[:, h:h + 1] - acst[h:h + 1, :]
                dec = jnp.exp(jnp.where(causal, seg, -jnp.inf))
                ms.append((cb * dec).astype(BF16))
                rs.append(xr_bf * head_masks[j])
            y_diag = _dot(jnp.concatenate(ms, axis=1), jnp.concatenate(rs, axis=0))
            st = st_scr[:, gs]
            y_off = _dot(cg, st.astype(BF16)) * xbc_scr[erow, gs]
            y_scr[crow, gs] = y_scr[crow, gs] + y_diag + y_off
            xrd = (xr * xbc_scr[erow, d_inner + g * GROUP_COLS:d_inner + (g + 1) * GROUP_COLS]).astype(BF16)
            cd = xbc_scr[pl.ds(CONV_PAD + r0 + q - 1, 1), gs]
            st_scr[:, gs] = st * cd + _dot(bmt_scr[c * n_groups + g], xrd)
        return carry

    lax.fori_loop(0, n_chunks, chunk_body, 0)

    @pl.when(t == last_t)
    def _ssm_state():
        for g in range(n_groups):
            cs = slice(g * GROUP_COLS, (g + 1) * GROUP_COLS)
            hout_ref[0, cs, :] = st_scr[:, cs].T

    for g in range(n_groups):
        gs = slice(g * GROUP_COLS, (g + 1) * GROUP_COLS)
        z = _dot(u_scr[...], wz_ref[:, gs])
        yn_scr[:, gs] = _rms(y_scr[:, gs] * _silu(z), gssd_ref[:, gs]).astype(BF16)
    y_ssd = _dot(yn_scr[...], wssd_ref[...])

    pos = pos0 + t * lt + lax.broadcasted_iota(jnp.int32, (lt, GROUP_COLS), 0)
    pool_parts = []
    for gi, w in enumerate(POOL_WINDOWS):
        gs = slice(gi * GROUP_COLS, (gi + 1) * GROUP_COLS)
        p_now = pp_scr[pl.ds(POOL_PAD, lt), gs]
        win = p_now
        for k in range(1, w):
            win = win + pp_scr[pl.ds(POOL_PAD - k, lt), gs]
        cnt = jnp.minimum(pos + 1, w).astype(F32)
        pm = win / cnt - p_now
        pool_parts.append(_dot(pm.astype(BF16), wpool_ref[gi]) * pscale_ref[:, gs])
    y_pool = jnp.concatenate(pool_parts, axis=1)

    @pl.when(t == last_t)
    def _pool_state():
        poolo_ref[0] = pp_scr[pl.ds(POOL_PAD + lv - POOL_HIST, POOL_HIST), :]

    pp_scr[0:POOL_PAD, :] = pp_scr[lt:lt + POOL_PAD, :]

    d_model = x.shape[-1]
    gate_ssd = jax.nn.sigmoid(_dot(u_scr[...], wg_ref[:, 0:d_model]))
    gate_pool = jax.nn.sigmoid(_dot(u_scr[...], wg_ref[:, d_model:2 * d_model]))
    merged = (gate_ssd * y_ssd + gate_pool * y_pool).astype(BF16)
    mix = _dot(merged, wo_ref[...])
    out_ref[0] = x + mod[2:3] * _rms(mix, gpost_ref[...])


def _mix(x, mod, conv_hist, pool_hist, h0, p, *, lt, lv, pos0):
    bsz, lpad, d = x.shape
    nt = lpad // lt
    d_inner = p["wz"].shape[1]
    conv_dim = p["wxbc"].shape[1]
    n_groups = (conv_dim - d_inner) // (2 * SSD_STATE)
    d_pool = p["wp"].shape[1]
    n_chunks = lt // SSD_CHUNK
    hp = h0.shape[1]

    def whole(a):
        return pl.BlockSpec(memory_space=pltpu.VMEM)

    weights = [p["gpre"], p["gpost_mix"], p["wz"], p["wxbc"], p["wdt"], p["wp"], p["wg"],
               p["convw"], p["convb"], p["dtb"], p["alog"], p["dskip"], p["gssd"], p["e2"],
               p["wssd"], p["wpool"], p["pscale"], p["wo"]]
    in_specs = [
        pl.BlockSpec((1, lt, d), lambda b, t: (b, t, 0)),
        pl.BlockSpec((1, 6, d), lambda b, t: (b, 0, 0)),
        pl.BlockSpec((1, CONV_PAD, conv_dim), lambda b, t: (b, 0, 0)),
        pl.BlockSpec((1, POOL_PAD, d_pool), lambda b, t: (b, 0, 0)),
        pl.BlockSpec((1, hp, SSD_STATE), lambda b, t: (b, 0, 0)),
    ] + [whole(w) for w in weights]
    out_shape = (
        jax.ShapeDtypeStruct((bsz, lpad, d), F32),
        jax.ShapeDtypeStruct((bsz, hp, SSD_STATE), F32),
        jax.ShapeDtypeStruct((bsz, CONV_WIDTH - 1, conv_dim), F32),
        jax.ShapeDtypeStruct((bsz, POOL_HIST, d_pool), F32),
    )
    out_specs = (
        pl.BlockSpec((1, lt, d), lambda b, t: (b, t, 0)),
        pl.BlockSpec((1, hp, SSD_STATE), lambda b, t: (b, 0, 0)),
        pl.BlockSpec((1, CONV_WIDTH - 1, conv_dim), lambda b, t: (b, 0, 0)),
        pl.BlockSpec((1, POOL_HIST, d_pool), lambda b, t: (b, 0, 0)),
    )
    scratch = [
        pltpu.VMEM((lt, d), BF16),
        pltpu.VMEM((CONV_PAD + lt, conv_dim), F32),
        pltpu.VMEM((POOL_PAD + lt, d_pool), F32),
        pltpu.VMEM((lt, d_inner), F32),
        pltpu.VMEM((lt, n_groups * SSD_STATE), BF16),
        pltpu.VMEM((n_chunks * n_groups, SSD_STATE, SSD_CHUNK), BF16),
        pltpu.VMEM((lt, n_groups * SSD_STATE), BF16),
        pltpu.VMEM((lt, LANES), F32),
        pltpu.VMEM((n_chunks, LANES, SSD_CHUNK), F32),
        pltpu.VMEM((lt, d_inner), F32),
        pltpu.VMEM((lt, d_inner), BF16),
        pltpu.VMEM((SSD_STATE, hp), F32),
    ]
    kern = functools.partial(_mix_kernel, lt=lt, lv=lv, pos0=pos0, d_inner=d_inner, n_groups=n_groups)
    return pl.pallas_call(
        kern, out_shape=out_shape, grid=(bsz, nt), in_specs=in_specs, out_specs=out_specs,
        scratch_shapes=scratch,
        compiler_params=pltpu.CompilerParams(dimension_semantics=("arbitrary", "arbitrary"),
                                             vmem_limit_bytes=VMEM_LIMIT),
        name="mix",
    )(x, mod, conv_hist, pool_hist, h0, *weights)


def _mlp_kernel(x_ref, mod_ref, gpre_ref, gpost_ref, wup_ref, wdown_ref, out_ref, v_scr):
    x = x_ref[0]
    mod = mod_ref[0]
    v_scr[...] = (_rms(x, gpre_ref[...]) * (1.0 + mod[4:5]) + mod[3:4]).astype(BF16)
    slab = 512
    acc = None
    for c0 in range(0, wup_ref.shape[1], slab):
        h = jnp.square(jnp.maximum(_dot(v_scr[...], wup_ref[:, c0:c0 + slab]), 0.0)).astype(BF16)
        part = _dot(h, wdown_ref[c0:c0 + slab, :])
        acc = part if acc is None else acc + part
    out_ref[0] = x + mod[5:6] * _rms(acc, gpost_ref[...])


def _mlp(x, mod, p, *, tm):
    bsz, lpad, d = x.shape
    nt = lpad // tm
    weights = [p["gpre_mlp"], p["gpost_mlp"], p["wup"], p["wdown"]]
    return pl.pallas_call(
        _mlp_kernel,
        out_shape=jax.ShapeDtypeStruct((bsz, lpad, d), F32),
        grid=(bsz, nt),
        in_specs=[pl.BlockSpec((1, tm, d), lambda b, t: (b, t, 0)),
                  pl.BlockSpec((1, 6, d), lambda b, t: (b, 0, 0))]
                 + [pl.BlockSpec(memory_space=pltpu.VMEM) for _ in weights],
        out_specs=pl.BlockSpec((1, tm, d), lambda b, t: (b, t, 0)),
        scratch_shapes=[pltpu.VMEM((tm, d), BF16)],
        compiler_params=pltpu.CompilerParams(dimension_semantics=("arbitrary", "arbitrary"),
                                             vmem_limit_bytes=VMEM_LIMIT),
        name="mlp",
    )(x, mod, *weights)


def _prep_weights(g_pre_mix, g_post_mix, g_pre_mlp, g_post_mlp, w_in, conv_w, conv_b, dt_bias, a_log,
                  d_skip, g_ssd_norm, w_ssd_out, w_pool_group, pool_scale, w_o, w_up, w_down):
    d = w_in.shape[0]
    d_inner = w_ssd_out.shape[0]
    conv_dim = conv_w.shape[1]
    n_heads = dt_bias.shape[0]
    d_pool = pool_scale.shape[0]
    s_z, s_xbc = d_inner, d_inner + conv_dim
    s_dt, s_pool = s_xbc + n_heads, s_xbc + n_heads + d_pool
    pad = LANES - n_heads
    head_of_col = jnp.arange(d_inner) // SSD_HEAD_DIM
    e1 = (jnp.arange(LANES)[:, None] == head_of_col[None, :]).astype(BF16)
    row = lambda a: a.reshape(1, -1).astype(F32)
    return dict(
        gpre=row(g_pre_mix), gpost_mix=row(g_post_mix), gpre_mlp=row(g_pre_mlp), gpost_mlp=row(g_post_mlp),
        wz=w_in[:, :s_z].astype(BF16), wxbc=w_in[:, s_z:s_xbc].astype(BF16),
        wdt=jnp.pad(w_in[:, s_xbc:s_dt], ((0, 0), (0, pad))).astype(BF16),
        wp=w_in[:, s_dt:s_pool].astype(BF16), wg=w_in[:, s_pool:].astype(BF16),
        convw=conv_w.astype(F32), convb=row(conv_b),
        dtb=jnp.pad(row(dt_bias), ((0, 0), (0, pad))), alog=jnp.pad(row(a_log), ((0, 0), (0, pad))),
        dskip=jnp.repeat(row(d_skip), SSD_HEAD_DIM, axis=1), gssd=row(g_ssd_norm),
        e2=jnp.concatenate([e1, e1], axis=0),
        wssd=w_ssd_out.astype(BF16), wpool=w_pool_group.astype(BF16), pscale=row(pool_scale),
        wo=w_o.astype(BF16), wup=w_up.astype(BF16), wdown=w_down.astype(BF16),
    )


def _layer(x, mod, conv_hist, pool_hist, h0, p, *, pos0):
    bsz, seq, d = x.shape
    heads, hdim, nstate = h0.shape[1:]
    if seq % MIX_TILE == 0:
        lt, lpad = MIX_TILE, seq
    else:
        lt = lpad = -(-seq // SSD_CHUNK) * SSD_CHUNK
    lv = seq - (lpad - lt)
    xp = x if lpad == seq else jnp.pad(x, ((0, 0), (0, lpad - seq), (0, 0)))
    convh = jnp.pad(conv_hist, ((0, 0), (CONV_PAD - (CONV_WIDTH - 1), 0), (0, 0)))
    poolh = jnp.pad(pool_hist, ((0, 0), (POOL_PAD - POOL_HIST, 0), (0, 0)))
    x1, h_new, conv_state, pool_state = _mix(xp, mod, convh, poolh, h0.reshape(bsz, heads * hdim, nstate), p,
                                             lt=lt, lv=lv, pos0=pos0)
    tm = MLP_TILE if lpad % MLP_TILE == 0 else lpad
    y = _mlp(x1, mod, p, tm=tm)
    if lpad != seq:
        y = y[:, :seq]
    return y, h_new.reshape(bsz, heads, hdim, nstate), conv_state, pool_state


def kernel(x_prompt, x_sample, state_ssm, state_conv, state_pool, c_prompt, c_sample, w_ada, b_ada, g_pre_mix, g_post_mix, g_pre_mlp, g_post_mlp, w_in, conv_w, conv_b, dt_bias, a_log, d_skip, g_ssd_norm, w_ssd_out, w_pool_group, pool_scale, w_o, w_up, w_down):
    yp, ys = x_prompt, x_sample
    bp, d = x_prompt.shape[0], x_prompt.shape[2]
    bs = x_sample.shape[0]
    depth = w_in.shape[0]
    outs = [[] for _ in range(6)]
    for l in range(depth):
        p = _prep_weights(g_pre_mix[l], g_post_mix[l], g_pre_mlp[l], g_post_mlp[l], w_in[l], conv_w[l],
                          conv_b[l], dt_bias[l], a_log[l], d_skip[l], g_ssd_norm[l], w_ssd_out[l],
                          w_pool_group[l], pool_scale[l], w_o[l], w_up[l], w_down[l])
        mod = _adaln(jnp.concatenate([c_prompt, c_sample], axis=0), w_ada[l], b_ada[l])
        mod_p = mod[:bp].reshape(bp, 6, d)
        mod_s = mod[bp:].reshape(bs, 6, d)
        zc = jnp.zeros((bp,) + state_conv.shape[2:], yp.dtype)
        zp = jnp.zeros((bp,) + state_pool.shape[2:], yp.dtype)
        zh = jnp.zeros((bp,) + state_ssm.shape[2:], yp.dtype)
        yp, h_p, c_p, p_p = _layer(yp, mod_p, zc, zp, zh, p, pos0=0)
        ys, h_s, c_s, p_s = _layer(ys, mod_s, state_conv[l], state_pool[l], state_ssm[l], p, pos0=PAST_LEN)
        for lst, val in zip(outs, (h_p, c_p, p_p, h_s, c_s, p_s)):
            lst.append(val)
    return (yp, ys) + tuple(jnp.stack(o) for o in outs)
```

```python
import functools

import jax
import jax.numpy as jnp
from jax import lax
from jax.experimental import pallas as pl
from jax.experimental.pallas import tpu as pltpu

F32 = jnp.float32
BF16 = jnp.bfloat16

EPS = 1e-6
LOG2E = 1.4426950408889634
PAST_LEN = 2048
POOL_WINDOWS = (2, 4, 8, 16)
POOL_HIST = max(POOL_WINDOWS) - 1
CONV_WIDTH = 4
SSD_HEAD_DIM = 64
SSD_STATE = 128
LANES = 128
SUBLANES = 8
SSD_CHUNK = 128
ROW_GROUPS = SSD_CHUNK // SUBLANES
GROUP_COLS = 256
HEADS_PER_GROUP = GROUP_COLS // SSD_HEAD_DIM
PROJ_COLS = 512
CONV_CARRY = (CONV_WIDTH - 1) * SUBLANES
POOL_CARRY = POOL_HIST * SUBLANES
MIX_TILE = 256
MLP_TILE = 512
VMEM_LIMIT = 60 * 1024 * 1024


def _row_of_token(tok):
    return (tok % ROW_GROUPS) * SUBLANES + tok // ROW_GROUPS


def _token_of_row(row):
    return (row % SUBLANES) * ROW_GROUPS + row // SUBLANES


def _dot(a, b):
    return jnp.dot(a, b, preferred_element_type=F32)


def _dot_nt(a, b):
    return lax.dot_general(a, b, (((1,), (1,)), ((), ())), preferred_element_type=F32)


def _sigmoid(v):
    return 1.0 / (1.0 + jnp.exp2(v * (-LOG2E)))


def _silu(v):
    return v * _sigmoid(v)


def _rms(v, g):
    return v * lax.rsqrt(jnp.mean(v * v, axis=-1, keepdims=True) + EPS) * g


def _adaln_kernel(c_ref, w_ref, b_ref, o_ref):
    c = c_ref[...]
    o_ref[...] = jnp.dot(_silu(c), w_ref[...], preferred_element_type=F32,
                         precision=lax.Precision.HIGHEST) + b_ref[...]


def _adaln(c, w_ada, b_ada):
    rows, d = c.shape
    n = w_ada.shape[1]
    bn = d
    return pl.pallas_call(
        _adaln_kernel,
        out_shape=jax.ShapeDtypeStruct((rows, n), F32),
        grid=(n // bn,),
        in_specs=[pl.BlockSpec((rows, d), lambda j: (0, 0)),
                  pl.BlockSpec((d, bn), lambda j: (0, j)),
                  pl.BlockSpec((1, bn), lambda j: (0, j))],
        out_specs=pl.BlockSpec((rows, bn), lambda j: (0, j)),
        compiler_params=pltpu.CompilerParams(dimension_semantics=("arbitrary",)),
        name="adaln",
    )(c, w_ada, b_ada.reshape(1, n))


def _shifted_groups(prev_tail, cur, n_back):
    first = lax.broadcasted_iota(jnp.int32, (SUBLANES, cur.shape[1]), 0) == 0
    tail = cur[SSD_CHUNK - n_back * SUBLANES:]
    parts = []
    for j in range(n_back):
        rs = slice(j * SUBLANES, (j + 1) * SUBLANES)
        parts.append(jnp.where(first, pltpu.roll(prev_tail[rs], 1, axis=0), pltpu.roll(tail[rs], 1, axis=0)))
    return parts


def _mix_kernel(x_ref, mod_ref, convh_ref, poolh_ref, h0_ref,
                gpre_ref, gpost_ref, wz_ref, wxbc_ref, wdt_ref, wp_ref, wg_ref,
                convw_ref, convb_ref, dtb_ref, alog_ref, dskip_ref, gssd_ref, e2_ref,
                wssd_ref, wpool_ref, pscale_ref, wo_ref,
                out_ref, hout_ref, convo_ref, poolo_ref,
                perm_scr, u_scr, convc_scr, poolc_scr, xr_scr, bm_scr, bmt_scr, cm_scr, acs_scr, acst_scr,
                ea_scr, dte_scr, y_scr, yn_scr, ypool_scr, st_scr,
                *, lt, lv, pos0, d_inner, n_groups):
    q = SSD_CHUNK
    n_chunks = lt // q
    t = pl.program_id(1)
    last_t = pl.num_programs(1) - 1
    d_bc = n_groups * SSD_STATE
    d_model = x_ref.shape[-1]
    n_slabs = d_model // LANES

    @pl.when(t == 0)
    def _init():
        convc_scr[...] = convh_ref[0]
        poolc_scr[...] = poolh_ref[0]
        for g in range(n_groups):
            cs = slice(g * GROUP_COLS, (g + 1) * GROUP_COLS)
            st_scr[:, cs] = h0_ref[0, cs, :].T

    x = x_ref[0]
    mod = mod_ref[0]
    u = _rms(x, gpre_ref[...]) * (1.0 + mod[1:2]) + mod[0:1]
    for c in range(n_chunks):
        for i in range(ROW_GROUPS):
            start = c * q + (i // 2 if i % 2 == 0 else q // 2 + i // 2)
            r0 = c * q + i * SUBLANES
            for j in range(n_slabs):
                perm_scr[j, pl.ds(start, SUBLANES, stride=SUBLANES), :] = (
                    u[r0:r0 + SUBLANES, j * LANES:(j + 1) * LANES])
    for j in range(n_slabs):
        u_scr[:, j * LANES:(j + 1) * LANES] = perm_scr[j].astype(BF16)

    def tokens(shape, axis):
        r = lax.broadcasted_iota(jnp.int32, shape, axis)
        return (r // q) * q + (r % SUBLANES) * ROW_GROUPS + (r % q) // SUBLANES

    lane = lax.broadcasted_iota(jnp.int32, (1, LANES), 1)
    n_heads = d_inner // SSD_HEAD_DIM
    dt = jax.nn.softplus(_dot(u_scr[...], wdt_ref[...]) + dtb_ref[...])
    if lv < lt:
        dt = jnp.where(tokens((lt, LANES), 0) < lv, dt, 0.0)
    a_row = jnp.where(lane < n_heads, -jnp.exp(alog_ref[...]), 0.0)
    da = dt * a_row
    causal = tokens((q, q), 0) >= tokens((q, q), 1)
    tri = causal.astype(F32)
    ea_parts, dte_parts = [], []
    for c in range(n_chunks):
        acs = jnp.dot(tri, da[c * q:(c + 1) * q], preferred_element_type=F32,
                      precision=lax.Precision.HIGHEST)
        acs_scr[c * q:(c + 1) * q, :] = acs * LOG2E
        acst_scr[c] = (acs * LOG2E).T
        ea_parts.append(jnp.exp(acs))
        dte_parts.append(jnp.exp(acs[q - 1:q, :] - acs))

    def split_hi_lo(v):
        hi = v.astype(BF16)
        lo = (v - hi.astype(F32)).astype(BF16)
        return jnp.concatenate([hi, lo], axis=1)

    dt2 = split_hi_lo(dt)
    ea2 = split_hi_lo(jnp.concatenate(ea_parts, axis=0))
    dte2 = split_hi_lo(jnp.concatenate(dte_parts, axis=0))
    for c0 in range(0, d_inner, PROJ_COLS):
        cs = slice(c0, c0 + PROJ_COLS)
        ea_scr[:, cs] = _dot(ea2, e2_ref[:, cs])
        dte_scr[:, cs] = _dot(dte2, e2_ref[:, cs])

    for p0 in range(0, d_inner + 2 * d_bc, PROJ_COLS):
        ps = slice(p0, p0 + PROJ_COLS)
        res_wide = _dot(u_scr[...], wxbc_ref[:, ps])
        dt_wide = _dot(dt2, e2_ref[:, ps]) if p0 < d_inner else None
        for h0 in range(0, PROJ_COLS, GROUP_COLS):
            c0 = p0 + h0
            cs = slice(c0, c0 + GROUP_COLS)
            res = res_wide[:, h0:h0 + GROUP_COLS]
            prev_tail = convc_scr[:, cs]
            for c in range(n_chunks):
                rows = slice(c * q, (c + 1) * q)
                cur = res[rows]
                ext = jnp.concatenate(_shifted_groups(prev_tail, cur, CONV_WIDTH - 1) + [cur], axis=0)
                acc = convb_ref[:, cs] + ext[CONV_CARRY:] * convw_ref[CONV_WIDTH - 1:CONV_WIDTH, cs]
                for k in range(1, CONV_WIDTH):
                    lo = CONV_CARRY - k * SUBLANES
                    acc = acc + ext[lo:lo + q] * convw_ref[CONV_WIDTH - 1 - k:CONV_WIDTH - k, cs]
                v = _silu(acc)
                if c0 < d_inner:
                    y_scr[rows, cs] = v * dskip_ref[:, cs]
                    xr_scr[rows, cs] = v * dt_wide[rows, h0:h0 + GROUP_COLS]
                elif c0 < d_inner + d_bc:
                    b0 = c0 - d_inner
                    bm_scr[rows, b0:b0 + GROUP_COLS] = v.astype(BF16)
                    for gg in range(GROUP_COLS // SSD_STATE):
                        g = b0 // SSD_STATE + gg
                        bmt_scr[c * n_groups + g] = v[:, gg * SSD_STATE:(gg + 1) * SSD_STATE].T.astype(BF16)
                else:
                    b0 = c0 - d_inner - d_bc
                    cm_scr[rows, b0:b0 + GROUP_COLS] = v.astype(BF16)
                prev_tail = cur[q - CONV_CARRY:]
            convc_scr[:, cs] = prev_tail
            for j in range(CONV_WIDTH - 1):
                tok = lv - (CONV_WIDTH - 1) + j
                r = (tok // q) * q + _row_of_token(tok % q)
                convo_ref[0, j:j + 1, cs] = res[r:r + 1]

    lane_g = lax.broadcasted_iota(jnp.int32, (1, GROUP_COLS), 1)
    head_masks = [((lane_g >= j * SSD_HEAD_DIM) & (lane_g < (j + 1) * SSD_HEAD_DIM)).astype(BF16)
                  for j in range(HEADS_PER_GROUP)]

    def chunk_body(c, carry):
        r0 = pl.multiple_of(c * q, q)
        crow = pl.ds(r0, q)
        acs = acs_scr[crow, :]
        acst = acst_scr[c]
        for g in range(n_groups):
            gs = slice(g * GROUP_COLS, (g + 1) * GROUP_COLS)
            ns = slice(g * SSD_STATE, (g + 1) * SSD_STATE)
            cg = cm_scr[crow, ns]
            cb = _dot_nt(cg, bm_scr[crow, ns])
            xr = xr_scr[crow, gs]
            xr_bf = xr.astype(BF16)
            ms, rs = [], []
            for j in range(HEADS_PER_GROUP):
                h = g * HEADS_PER_GROUP + j
                seg = acs[:, h:h + 1] - acst[h:h + 1, :]
                dec = jnp.exp2(jnp.where(causal, seg, -jnp.inf))
                ms.append((cb * dec).astype(BF16))
                rs.append(xr_bf * head_masks[j])
            y_diag = _dot(jnp.concatenate(ms, axis=1), jnp.concatenate(rs, axis=0))
            st = st_scr[:, gs]
            y_off = _dot(cg, st.astype(BF16)) * ea_scr[crow, gs]
            y_scr[crow, gs] = y_scr[crow, gs] + y_diag + y_off
            xrd = (xr * dte_scr[crow, gs]).astype(BF16)
            cd = ea_scr[pl.ds(r0 + q - 1, 1), gs]
            st_scr[:, gs] = st * cd + _dot(bmt_scr[c * n_groups + g], xrd)
        return carry

    lax.fori_loop(0, n_chunks, chunk_body, 0)

    @pl.when(t == last_t)
    def _ssm_state():
        for g in range(n_groups):
            cs = slice(g * GROUP_COLS, (g + 1) * GROUP_COLS)
            hout_ref[0, cs, :] = st_scr[:, cs].T

    for p0 in range(0, d_inner, PROJ_COLS):
        z_wide = _dot(u_scr[...], wz_ref[:, p0:p0 + PROJ_COLS])
        for h0 in range(0, PROJ_COLS, GROUP_COLS):
            gs = slice(p0 + h0, p0 + h0 + GROUP_COLS)
            yn_scr[:, gs] = _rms(y_scr[:, gs] * _silu(z_wide[:, h0:h0 + GROUP_COLS]), gssd_ref[:, gs]).astype(BF16)
    y_ssd = _dot(yn_scr[...], wssd_ref[...])

    p_all = _dot(u_scr[...], wp_ref[...])
    for gi, w in enumerate(POOL_WINDOWS):
        gs = slice(gi * GROUP_COLS, (gi + 1) * GROUP_COLS)
        res = p_all[:, gs]
        prev_tail = poolc_scr[:, gs]
        for c in range(n_chunks):
            rows = slice(c * q, (c + 1) * q)
            cur = res[rows]
            back = _shifted_groups(prev_tail[POOL_CARRY - (w - 1) * SUBLANES:], cur, w - 1)
            win = jnp.concatenate(back + [cur], axis=0)
            span = 1
            while span < w:
                win = win[span * SUBLANES:] + win[:win.shape[0] - span * SUBLANES]
                span *= 2
            if pos0 >= POOL_HIST or c > 0:
                pm = win * (1.0 / w) - cur
            else:
                cnt = jnp.minimum(tokens((q, GROUP_COLS), 0) + (pos0 + 1), w).astype(F32)
                inv = jnp.where(t == 0, 1.0 / cnt, 1.0 / w)
                pm = win * inv - cur
            ypool_scr[rows, gs] = _dot(pm.astype(BF16), wpool_ref[gi]) * pscale_ref[:, gs]
            prev_tail = cur[SUBLANES:]
        poolc_scr[:, gs] = prev_tail
        for j in range(POOL_HIST):
            tok = lv - POOL_HIST + j
            r = (tok // q) * q + _row_of_token(tok % q)
            poolo_ref[0, j:j + 1, gs] = res[r:r + 1]

    gate_ssd = _sigmoid(_dot(u_scr[...], wg_ref[:, 0:d_model]))
    gate_pool = _sigmoid(_dot(u_scr[...], wg_ref[:, d_model:2 * d_model]))
    merged = (gate_ssd * y_ssd + gate_pool * ypool_scr[...]).astype(BF16)
    mix = _dot(merged, wo_ref[...])
    branch = mod[2:3] * _rms(mix, gpost_ref[...])
    for c in range(n_chunks):
        for v in range(ROW_GROUPS):
            r0 = c * q + v * SUBLANES
            for j in range(n_slabs):
                perm_scr[j, pl.ds(c * q + v, SUBLANES, stride=ROW_GROUPS), :] = (
                    branch[r0:r0 + SUBLANES, j * LANES:(j + 1) * LANES])
    for j in range(n_slabs):
        ls = slice(j * LANES, (j + 1) * LANES)
        out_ref[0, :, ls] = x[:, ls] + perm_scr[j]


def _mix(x, mod, conv_hist, pool_hist, h0, p, *, lt, lv, pos0):
    bsz, lpad, d = x.shape
    nt = lpad // lt
    d_inner = p["wz"].shape[1]
    conv_dim = p["wxbc"].shape[1]
    n_groups = (conv_dim - d_inner) // (2 * SSD_STATE)
    d_pool = p["wp"].shape[1]
    n_chunks = lt // SSD_CHUNK
    hp = h0.shape[1]
    assert lv >= POOL_HIST and lt % SSD_CHUNK == 0 and d_inner == n_groups * GROUP_COLS

    weights = [p["gpre"], p["gpost_mix"], p["wz"], p["wxbc"], p["wdt"], p["wp"], p["wg"],
               p["convw"], p["convb"], p["dtb"], p["alog"], p["dskip"], p["gssd"], p["e2"],
               p["wssd"], p["wpool"], p["pscale"], p["wo"]]
    in_specs = [
        pl.BlockSpec((1, lt, d), lambda b, t: (b, t, 0)),
        pl.BlockSpec((1, 6, d), lambda b, t: (b, 0, 0)),
        pl.BlockSpec((1, CONV_CARRY, conv_dim), lambda b, t: (b, 0, 0)),
        pl.BlockSpec((1, POOL_CARRY, d_pool), lambda b, t: (b, 0, 0)),
        pl.BlockSpec((1, hp, SSD_STATE), lambda b, t: (b, 0, 0)),
    ] + [pl.BlockSpec(memory_space=pltpu.VMEM) for _ in weights]
    out_shape = (
        jax.ShapeDtypeStruct((bsz, lpad, d), F32),
        jax.ShapeDtypeStruct((bsz, hp, SSD_STATE), F32),
        jax.ShapeDtypeStruct((bsz, CONV_WIDTH - 1, conv_dim), F32),
        jax.ShapeDtypeStruct((bsz, POOL_HIST, d_pool), F32),
    )
    out_specs = (
        pl.BlockSpec((1, lt, d), lambda b, t: (b, t, 0)),
        pl.BlockSpec((1, hp, SSD_STATE), lambda b, t: (b, 0, 0)),
        pl.BlockSpec((1, CONV_WIDTH - 1, conv_dim), lambda b, t: (b, 0, 0)),
        pl.BlockSpec((1, POOL_HIST, d_pool), lambda b, t: (b, 0, 0)),
    )
    scratch = [
        pltpu.VMEM((d // LANES, lt, LANES), F32),
        pltpu.VMEM((lt, d), BF16),
        pltpu.VMEM((CONV_CARRY, conv_dim), F32),
        pltpu.VMEM((POOL_CARRY, d_pool), F32),
        pltpu.VMEM((lt, d_inner), F32),
        pltpu.VMEM((lt, n_groups * SSD_STATE), BF16),
        pltpu.VMEM((n_chunks * n_groups, SSD_STATE, SSD_CHUNK), BF16),
        pltpu.VMEM((lt, n_groups * SSD_STATE), BF16),
        pltpu.VMEM((lt, LANES), F32),
        pltpu.VMEM((n_chunks, LANES, SSD_CHUNK), F32),
        pltpu.VMEM((lt, d_inner), F32),
        pltpu.VMEM((lt, d_inner), F32),
        pltpu.VMEM((lt, d_inner), F32),
        pltpu.VMEM((lt, d_inner), BF16),
        pltpu.VMEM((lt, d_pool), F32),
        pltpu.VMEM((SSD_STATE, hp), F32),
    ]
    kern = functools.partial(_mix_kernel, lt=lt, lv=lv, pos0=pos0, d_inner=d_inner, n_groups=n_groups)
    return pl.pallas_call(
        kern, out_shape=out_shape, grid=(bsz, nt), in_specs=in_specs, out_specs=out_specs,
        scratch_shapes=scratch,
        compiler_params=pltpu.CompilerParams(dimension_semantics=("arbitrary", "arbitrary"),
                                             vmem_limit_bytes=VMEM_LIMIT),
        name="mix",
    )(x, mod, conv_hist, pool_hist, h0, *weights)


def _mlp_kernel(x_ref, mod_ref, gpre_ref, gpost_ref, wup_ref, wdown_ref, out_ref, v_scr):
    x = x_ref[0]
    mod = mod_ref[0]
    v_scr[...] = (_rms(x, gpre_ref[...]) * (1.0 + mod[4:5]) + mod[3:4]).astype(BF16)
    slab = 512
    acc = None
    for c0 in range(0, wup_ref.shape[1], slab):
        h = jnp.square(jnp.maximum(_dot(v_scr[...], wup_ref[:, c0:c0 + slab]), 0.0)).astype(BF16)
        part = _dot(h, wdown_ref[c0:c0 + slab, :])
        acc = part if acc is None else acc + part
    out_ref[0] = x + mod[5:6] * _rms(acc, gpost_ref[...])


def _mlp(x, mod, p, *, tm):
    bsz, lpad, d = x.shape
    nt = lpad // tm
    weights = [p["gpre_mlp"], p["gpost_mlp"], p["wup"], p["wdown"]]
    return pl.pallas_call(
        _mlp_kernel,
        out_shape=jax.ShapeDtypeStruct((bsz, lpad, d), F32),
        grid=(bsz, nt),
        in_specs=[pl.BlockSpec((1, tm, d), lambda b, t: (b, t, 0)),
                  pl.BlockSpec((1, 6, d), lambda b, t: (b, 0, 0))]
                 + [pl.BlockSpec(memory_space=pltpu.VMEM) for _ in weights],
        out_specs=pl.BlockSpec((1, tm, d), lambda b, t: (b, t, 0)),
        scratch_shapes=[pltpu.VMEM((tm, d), BF16)],
        compiler_params=pltpu.CompilerParams(dimension_semantics=("arbitrary", "arbitrary"),
                                             vmem_limit_bytes=VMEM_LIMIT),
        name="mlp",
    )(x, mod, *weights)


def _prep_weights(g_pre_mix, g_post_mix, g_pre_mlp, g_post_mlp, w_in, conv_w, conv_b, dt_bias, a_log,
                  d_skip, g_ssd_norm, w_ssd_out, w_pool_group, pool_scale, w_o, w_up, w_down):
    d_inner = w_ssd_out.shape[0]
    conv_dim = conv_w.shape[1]
    n_heads = dt_bias.shape[0]
    d_pool = pool_scale.shape[0]
    s_z, s_xbc = d_inner, d_inner + conv_dim
    s_dt, s_pool = s_xbc + n_heads, s_xbc + n_heads + d_pool
    pad = LANES - n_heads
    head_of_col = jnp.arange(d_inner) // SSD_HEAD_DIM
    e1 = (jnp.arange(LANES)[:, None] == head_of_col[None, :]).astype(BF16)
    row = lambda a: a.reshape(1, -1).astype(F32)
    return dict(
        gpre=row(g_pre_mix), gpost_mix=row(g_post_mix), gpre_mlp=row(g_pre_mlp), gpost_mlp=row(g_post_mlp),
        wz=w_in[:, :s_z].astype(BF16), wxbc=w_in[:, s_z:s_xbc].astype(BF16),
        wdt=jnp.pad(w_in[:, s_xbc:s_dt], ((0, 0), (0, pad))).astype(BF16),
        wp=w_in[:, s_dt:s_pool].astype(BF16), wg=w_in[:, s_pool:].astype(BF16),
        convw=conv_w.astype(F32), convb=row(conv_b),
        dtb=jnp.pad(row(dt_bias), ((0, 0), (0, pad))), alog=jnp.pad(row(a_log), ((0, 0), (0, pad))),
        dskip=jnp.repeat(row(d_skip), SSD_HEAD_DIM, axis=1), gssd=row(g_ssd_norm),
        e2=jnp.concatenate([e1, e1], axis=0),
        wssd=w_ssd_out.astype(BF16), wpool=w_pool_group.astype(BF16), pscale=row(pool_scale),
        wo=w_o.astype(BF16), wup=w_up.astype(BF16), wdown=w_down.astype(BF16),
    )


def _history_rows(hist):
    bsz, n, c = hist.shape
    out = jnp.zeros((bsz, n, SUBLANES, c), hist.dtype).at[:, :, SUBLANES - 1, :].set(hist)
    return out.reshape(bsz, n * SUBLANES, c)


def _layer(x, mod, conv_hist, pool_hist, h0, p, *, pos0):
    bsz, seq, d = x.shape
    heads, hdim, nstate = h0.shape[1:]
    if seq % MIX_TILE == 0:
        lt, lpad = MIX_TILE, seq
    else:
        lt = lpad = -(-seq // SSD_CHUNK) * SSD_CHUNK
    lv = seq - (lpad - lt)
    xp = x if lpad == seq else jnp.pad(x, ((0, 0), (0, lpad - seq), (0, 0)))
    x1, h_new, conv_state, pool_state = _mix(xp, mod, _history_rows(conv_hist), _history_rows(pool_hist),
                                             h0.reshape(bsz, heads * hdim, nstate), p,
                                             lt=lt, lv=lv, pos0=pos0)
    tm = MLP_TILE if lpad % MLP_TILE == 0 else lpad
    y = _mlp(x1, mod, p, tm=tm)
    if lpad != seq:
        y = y[:, :seq]
    return y, h_new.reshape(bsz, heads, hdim, nstate), conv_state, pool_state


def kernel(x_prompt, x_sample, state_ssm, state_conv, state_pool, c_prompt, c_sample, w_ada, b_ada, g_pre_mix, g_post_mix, g_pre_mlp, g_post_mlp, w_in, conv_w, conv_b, dt_bias, a_log, d_skip, g_ssd_norm, w_ssd_out, w_pool_group, pool_scale, w_o, w_up, w_down):
    yp, ys = x_prompt, x_sample
    bp, d = x_prompt.shape[0], x_prompt.shape[2]
    bs = x_sample.shape[0]
    depth = w_in.shape[0]
    outs = [[] for _ in range(6)]
    for l in range(depth):
        p = _prep_weights(g_pre_mix[l], g_post_mix[l], g_pre_mlp[l], g_post_mlp[l], w_in[l], conv_w[l],
                          conv_b[l], dt_bias[l], a_log[l], d_skip[l], g_ssd_norm[l], w_ssd_out[l],
                          w_pool_group[l], pool_scale[l], w_o[l], w_up[l], w_down[l])
        mod = _adaln(jnp.concatenate([c_prompt, c_sample], axis=0), w_ada[l], b_ada[l])
        mod_p = mod[:bp].reshape(bp, 6, d)
        mod_s = mod[bp:].reshape(bs, 6, d)
        zc = jnp.zeros((bp,) + state_conv.shape[2:], yp.dtype)
        zp = jnp.zeros((bp,) + state_pool.shape[2:], yp.dtype)
        zh = jnp.zeros((bp,) + state_ssm.shape[2:], yp.dtype)
        yp, h_p, c_p, p_p = _layer(yp, mod_p, zc, zp, zh, p, pos0=0)
        ys, h_s, c_s, p_s = _layer(ys, mod_s, state_conv[l], state_pool[l], state_ssm[l], p, pos0=PAST_LEN)
        for lst, val in zip(outs, (h_p, c_p, p_p, h_s, c_s, p_s)):
            lst.append(val)
    return (yp, ys) + tuple(jnp.stack(o) for o in outs)
```

```python
import functools

import jax
import jax.numpy as jnp
from jax import lax
from jax.experimental import pallas as pl
from jax.experimental.pallas import tpu as pltpu

F32 = jnp.float32
BF16 = jnp.bfloat16

EPS = 1e-6
LOG2E = 1.4426950408889634
PAST_LEN = 2048
POOL_WINDOWS = (2, 4, 8, 16)
POOL_HIST = max(POOL_WINDOWS) - 1
CONV_WIDTH = 4
SSD_HEAD_DIM = 64
SSD_STATE = 128
LANES = 128
SUBLANES = 8
SSD_CHUNK = 128
ROW_GROUPS = SSD_CHUNK // SUBLANES
GROUP_COLS = 256
HEADS_PER_GROUP = GROUP_COLS // SSD_HEAD_DIM
PROJ_COLS = 512
CONV_CARRY = (CONV_WIDTH - 1) * SUBLANES
POOL_CARRY = POOL_HIST * SUBLANES
MIX_TILE = 512
MLP_TILE = 512
VMEM_LIMIT = 60 * 1024 * 1024


def _row_of_token(tok):
    return (tok % ROW_GROUPS) * SUBLANES + tok // ROW_GROUPS


def _tile_row(tok):
    return (tok // SSD_CHUNK) * SSD_CHUNK + _row_of_token(tok % SSD_CHUNK)


def _tokens(shape, axis):
    r = lax.broadcasted_iota(jnp.int32, shape, axis)
    return (r // SSD_CHUNK) * SSD_CHUNK + (r % SUBLANES) * ROW_GROUPS + (r % SSD_CHUNK) // SUBLANES


def _dot(a, b):
    return jnp.dot(a, b, preferred_element_type=F32)


def _dot_nt(a, b):
    return lax.dot_general(a, b, (((1,), (1,)), ((), ())), preferred_element_type=F32)


def _sigmoid(v):
    return 1.0 / (1.0 + jnp.exp2(v * (-LOG2E)))


def _silu(v):
    return v * _sigmoid(v)


def _rms(v, g):
    return v * lax.rsqrt(jnp.mean(v * v, axis=-1, keepdims=True) + EPS) * g


def _shifted_groups(prev_tail, cur, n_back):
    first = lax.broadcasted_iota(jnp.int32, (SUBLANES, cur.shape[1]), 0) == 0
    tail = cur[SSD_CHUNK - n_back * SUBLANES:]
    parts = []
    for j in range(n_back):
        rs = slice(j * SUBLANES, (j + 1) * SUBLANES)
        parts.append(jnp.where(first, pltpu.roll(prev_tail[rs], 1, axis=0), pltpu.roll(tail[rs], 1, axis=0)))
    return parts


def _adaln_kernel(c_ref, w_ref, b_ref, o_ref):
    c = c_ref[...]
    o_ref[...] = jnp.dot(_silu(c), w_ref[...], preferred_element_type=F32,
                         precision=lax.Precision.HIGHEST) + b_ref[...]


def _adaln(c, w_ada, b_ada):
    rows, d = c.shape
    n = w_ada.shape[1]
    bn = d
    return pl.pallas_call(
        _adaln_kernel,
        out_shape=jax.ShapeDtypeStruct((rows, n), F32),
        grid=(n // bn,),
        in_specs=[pl.BlockSpec((rows, d), lambda j: (0, 0)),
                  pl.BlockSpec((d, bn), lambda j: (0, j)),
                  pl.BlockSpec((1, bn), lambda j: (0, j))],
        out_specs=pl.BlockSpec((rows, bn), lambda j: (0, j)),
        compiler_params=pltpu.CompilerParams(dimension_semantics=("arbitrary",)),
        name="adaln",
    )(c, w_ada, b_ada.reshape(1, n))


def _ssd_kernel(x_ref, mod_ref, convh_ref, h0_ref,
                gpre_ref, wz_ref, wxbc_ref, wdt_ref, convw_ref, convb_ref, dtb_ref, alog_ref, dskip_ref,
                gssd_ref, e2_ref,
                u_ref, yn_ref, hout_ref, convo_ref,
                perm_scr, convc_scr, xr_scr, bm_scr, bmt_scr, cm_scr, acs_scr, acst_scr,
                ea_scr, dte_scr, y_scr, st_scr,
                *, lt, lv, d_inner, n_groups):
    q = SSD_CHUNK
    n_chunks = lt // q
    t = pl.program_id(1)
    last_t = pl.num_programs(1) - 1
    d_bc = n_groups * SSD_STATE
    d_model = x_ref.shape[-1]
    n_slabs = d_model // LANES

    @pl.when(t == 0)
    def _init():
        convc_scr[...] = convh_ref[0]
        for g in range(n_groups):
            cs = slice(g * GROUP_COLS, (g + 1) * GROUP_COLS)
            st_scr[:, cs] = h0_ref[0, cs, :].T

    mod = mod_ref[0]
    u = _rms(x_ref[0], gpre_ref[...]) * (1.0 + mod[1:2]) + mod[0:1]
    for c in range(n_chunks):
        for i in range(ROW_GROUPS):
            start = c * q + (i // 2 if i % 2 == 0 else q // 2 + i // 2)
            r0 = c * q + i * SUBLANES
            for j in range(n_slabs):
                perm_scr[j, pl.ds(start, SUBLANES, stride=SUBLANES), :] = (
                    u[r0:r0 + SUBLANES, j * LANES:(j + 1) * LANES])
    for j in range(n_slabs):
        u_ref[0, :, j * LANES:(j + 1) * LANES] = perm_scr[j].astype(BF16)

    lane = lax.broadcasted_iota(jnp.int32, (1, LANES), 1)
    n_heads = d_inner // SSD_HEAD_DIM
    dt = jax.nn.softplus(_dot(u_ref[0], wdt_ref[...]) + dtb_ref[...])
    if lv < lt:
        dt = jnp.where(_tokens((lt, LANES), 0) < lv, dt, 0.0)
    a_row = jnp.where(lane < n_heads, -jnp.exp(alog_ref[...]), 0.0)
    da = dt * a_row
    causal = _tokens((q, q), 0) >= _tokens((q, q), 1)
    tri = causal.astype(F32)
    ea_parts, dte_parts = [], []
    for c in range(n_chunks):
        acs = jnp.dot(tri, da[c * q:(c + 1) * q], preferred_element_type=F32,
                      precision=lax.Precision.HIGHEST)
        acs_scr[c * q:(c + 1) * q, :] = acs * LOG2E
        acst_scr[c] = (acs * LOG2E).T
        ea_parts.append(jnp.exp(acs))
        dte_parts.append(jnp.exp(acs[q - 1:q, :] - acs))

    def split_hi_lo(v):
        hi = v.astype(BF16)
        lo = (v - hi.astype(F32)).astype(BF16)
        return jnp.concatenate([hi, lo], axis=1)

    dt2 = split_hi_lo(dt)
    ea2 = split_hi_lo(jnp.concatenate(ea_parts, axis=0))
    dte2 = split_hi_lo(jnp.concatenate(dte_parts, axis=0))
    for c0 in range(0, d_inner, PROJ_COLS):
        cs = slice(c0, c0 + PROJ_COLS)
        ea_scr[:, cs] = _dot(ea2, e2_ref[:, cs])
        dte_scr[:, cs] = _dot(dte2, e2_ref[:, cs])

    for p0 in range(0, d_inner + 2 * d_bc, PROJ_COLS):
        ps = slice(p0, p0 + PROJ_COLS)
        res_wide = _dot(u_ref[0], wxbc_ref[:, ps])
        dt_wide = _dot(dt2, e2_ref[:, ps]) if p0 < d_inner else None
        for h0 in range(0, PROJ_COLS, GROUP_COLS):
            c0 = p0 + h0
            cs = slice(c0, c0 + GROUP_COLS)
            res = res_wide[:, h0:h0 + GROUP_COLS]
            prev_tail = convc_scr[:, cs]
            for c in range(n_chunks):
                rows = slice(c * q, (c + 1) * q)
                cur = res[rows]
                ext = jnp.concatenate(_shifted_groups(prev_tail, cur, CONV_WIDTH - 1) + [cur], axis=0)
                acc = convb_ref[:, cs] + ext[CONV_CARRY:] * convw_ref[CONV_WIDTH - 1:CONV_WIDTH, cs]
                for k in range(1, CONV_WIDTH):
                    lo = CONV_CARRY - k * SUBLANES
                    acc = acc + ext[lo:lo + q] * convw_ref[CONV_WIDTH - 1 - k:CONV_WIDTH - k, cs]
                v = _silu(acc)
                if c0 < d_inner:
                    y_scr[rows, cs] = v * dskip_ref[:, cs]
                    xr_scr[rows, cs] = v * dt_wide[rows, h0:h0 + GROUP_COLS]
                elif c0 < d_inner + d_bc:
                    b0 = c0 - d_inner
                    bm_scr[rows, b0:b0 + GROUP_COLS] = v.astype(BF16)
                    for gg in range(GROUP_COLS // SSD_STATE):
                        g = b0 // SSD_STATE + gg
                        bmt_scr[c * n_groups + g] = v[:, gg * SSD_STATE:(gg + 1) * SSD_STATE].T.astype(BF16)
                else:
                    b0 = c0 - d_inner - d_bc
                    cm_scr[rows, b0:b0 + GROUP_COLS] = v.astype(BF16)
                prev_tail = cur[q - CONV_CARRY:]
            convc_scr[:, cs] = prev_tail
            for j in range(CONV_WIDTH - 1):
                r = _tile_row(lv - (CONV_WIDTH - 1) + j)
                convo_ref[0, j:j + 1, cs] = res[r:r + 1]

    lane_g = lax.broadcasted_iota(jnp.int32, (1, GROUP_COLS), 1)
    head_masks = [((lane_g >= j * SSD_HEAD_DIM) & (lane_g < (j + 1) * SSD_HEAD_DIM)).astype(BF16)
                  for j in range(HEADS_PER_GROUP)]

    def chunk_body(c, carry):
        r0 = pl.multiple_of(c * q, q)
        crow = pl.ds(r0, q)
        acs = acs_scr[crow, :]
        acst = acst_scr[c]
        for g in range(n_groups):
            gs = slice(g * GROUP_COLS, (g + 1) * GROUP_COLS)
            ns = slice(g * SSD_STATE, (g + 1) * SSD_STATE)
            cg = cm_scr[crow, ns]
            cb = _dot_nt(cg, bm_scr[crow, ns])
            xr = xr_scr[crow, gs]
            xr_bf = xr.astype(BF16)
            ms, rs = [], []
            for j in range(HEADS_PER_GROUP):
                h = g * HEADS_PER_GROUP + j
                seg = acs[:, h:h + 1] - acst[h:h + 1, :]
                dec = jnp.exp2(jnp.where(causal, seg, -jnp.inf))
                ms.append((cb * dec).astype(BF16))
                rs.append(xr_bf * head_masks[j])
            y_diag = _dot(jnp.concatenate(ms, axis=1), jnp.concatenate(rs, axis=0))
            st = st_scr[:, gs]
            y_off = _dot(cg, st.astype(BF16)) * ea_scr[crow, gs]
            y_scr[crow, gs] = y_scr[crow, gs] + y_diag + y_off
            xrd = (xr * dte_scr[crow, gs]).astype(BF16)
            cd = ea_scr[pl.ds(r0 + q - 1, 1), gs]
            st_scr[:, gs] = st * cd + _dot(bmt_scr[c * n_groups + g], xrd)
        return carry

    lax.fori_loop(0, n_chunks, chunk_body, 0)

    @pl.when(t == last_t)
    def _ssm_state():
        for g in range(n_groups):
            cs = slice(g * GROUP_COLS, (g + 1) * GROUP_COLS)
            hout_ref[0, cs, :] = st_scr[:, cs].T

    for p0 in range(0, d_inner, PROJ_COLS):
        z_wide = _dot(u_ref[0], wz_ref[:, p0:p0 + PROJ_COLS])
        for h0 in range(0, PROJ_COLS, GROUP_COLS):
            gs = slice(p0 + h0, p0 + h0 + GROUP_COLS)
            yn_ref[0, :, gs] = _rms(y_scr[:, gs] * _silu(z_wide[:, h0:h0 + GROUP_COLS]),
                                    gssd_ref[:, gs]).astype(BF16)


def _ssd(x, mod, conv_hist, h0, p, *, lt, lv):
    bsz, lpad, d = x.shape
    nt = lpad // lt
    d_inner = p["wz"].shape[1]
    conv_dim = p["wxbc"].shape[1]
    n_groups = (conv_dim - d_inner) // (2 * SSD_STATE)
    n_chunks = lt // SSD_CHUNK
    hp = h0.shape[1]
    assert lv >= CONV_WIDTH - 1 and lt % SSD_CHUNK == 0 and d_inner == n_groups * GROUP_COLS

    weights = [p["gpre"], p["wz"], p["wxbc"], p["wdt"], p["convw"], p["convb"], p["dtb"], p["alog"],
               p["dskip"], p["gssd"], p["e2"]]
    in_specs = [
        pl.BlockSpec((1, lt, d), lambda b, t: (b, t, 0)),
        pl.BlockSpec((1, 6, d), lambda b, t: (b, 0, 0)),
        pl.BlockSpec((1, CONV_CARRY, conv_dim), lambda b, t: (b, 0, 0)),
        pl.BlockSpec((1, hp, SSD_STATE), lambda b, t: (b, 0, 0)),
    ] + [pl.BlockSpec(memory_space=pltpu.VMEM) for _ in weights]
    out_shape = (
        jax.ShapeDtypeStruct((bsz, lpad, d), BF16),
        jax.ShapeDtypeStruct((bsz, lpad, d_inner), BF16),
        jax.ShapeDtypeStruct((bsz, hp, SSD_STATE), F32),
        jax.ShapeDtypeStruct((bsz, CONV_WIDTH - 1, conv_dim), F32),
    )
    out_specs = (
        pl.BlockSpec((1, lt, d), lambda b, t: (b, t, 0)),
        pl.BlockSpec((1, lt, d_inner), lambda b, t: (b, t, 0)),
        pl.BlockSpec((1, hp, SSD_STATE), lambda b, t: (b, 0, 0)),
        pl.BlockSpec((1, CONV_WIDTH - 1, conv_dim), lambda b, t: (b, 0, 0)),
    )
    scratch = [
        pltpu.VMEM((d // LANES, lt, LANES), F32),
        pltpu.VMEM((CONV_CARRY, conv_dim), F32),
        pltpu.VMEM((lt, d_inner), F32),
        pltpu.VMEM((lt, n_groups * SSD_STATE), BF16),
        pltpu.VMEM((n_chunks * n_groups, SSD_STATE, SSD_CHUNK), BF16),
        pltpu.VMEM((lt, n_groups * SSD_STATE), BF16),
        pltpu.VMEM((lt, LANES), F32),
        pltpu.VMEM((n_chunks, LANES, SSD_CHUNK), F32),
        pltpu.VMEM((lt, d_inner), F32),
        pltpu.VMEM((lt, d_inner), F32),
        pltpu.VMEM((lt, d_inner), F32),
        pltpu.VMEM((SSD_STATE, hp), F32),
    ]
    kern = functools.partial(_ssd_kernel, lt=lt, lv=lv, d_inner=d_inner, n_groups=n_groups)
    return pl.pallas_call(
        kern, out_shape=out_shape, grid=(bsz, nt), in_specs=in_specs, out_specs=out_specs,
        scratch_shapes=scratch,
        compiler_params=pltpu.CompilerParams(dimension_semantics=("arbitrary", "arbitrary"),
                                             vmem_limit_bytes=VMEM_LIMIT),
        name="ssd",
    )(x, mod, conv_hist, h0, *weights)


def _merge_kernel(x_ref, u_ref, yn_ref, mod_ref, poolh_ref,
                  gpost_ref, wp_ref, wg_ref, wssd_ref, wpool_ref, pscale_ref, wo_ref,
                  out_ref, poolo_ref,
                  perm_scr, poolc_scr, ypool_scr,
                  *, lt, lv, pos0):
    q = SSD_CHUNK
    n_chunks = lt // q
    t = pl.program_id(1)
    d_model = x_ref.shape[-1]
    n_slabs = d_model // LANES

    @pl.when(t == 0)
    def _init():
        poolc_scr[...] = poolh_ref[0]

    p_all = _dot(u_ref[0], wp_ref[...])
    for gi, w in enumerate(POOL_WINDOWS):
        gs = slice(gi * GROUP_COLS, (gi + 1) * GROUP_COLS)
        res = p_all[:, gs]
        prev_tail = poolc_scr[:, gs]
        for c in range(n_chunks):
            rows = slice(c * q, (c + 1) * q)
            cur = res[rows]
            back = _shifted_groups(prev_tail[POOL_CARRY - (w - 1) * SUBLANES:], cur, w - 1)
            win = jnp.concatenate(back + [cur], axis=0)
            span = 1
            while span < w:
                win = win[span * SUBLANES:] + win[:win.shape[0] - span * SUBLANES]
                span *= 2
            if pos0 >= POOL_HIST or c > 0:
                pm = win * (1.0 / w) - cur
            else:
                cnt = jnp.minimum(_tokens((q, GROUP_COLS), 0) + (pos0 + 1), w).astype(F32)
                inv = jnp.where(t == 0, 1.0 / cnt, 1.0 / w)
                pm = win * inv - cur
            ypool_scr[rows, gs] = _dot(pm.astype(BF16), wpool_ref[gi]) * pscale_ref[:, gs]
            prev_tail = cur[SUBLANES:]
        poolc_scr[:, gs] = prev_tail
        for j in range(POOL_HIST):
            r = _tile_row(lv - POOL_HIST + j)
            poolo_ref[0, j:j + 1, gs] = res[r:r + 1]

    y_ssd = _dot(yn_ref[0], wssd_ref[...])
    gate_ssd = _sigmoid(_dot(u_ref[0], wg_ref[:, 0:d_model]))
    gate_pool = _sigmoid(_dot(u_ref[0], wg_ref[:, d_model:2 * d_model]))
    merged = (gate_ssd * y_ssd + gate_pool * ypool_scr[...]).astype(BF16)
    mix = _dot(merged, wo_ref[...])
    branch = mod_ref[0][2:3] * _rms(mix, gpost_ref[...])
    for c in range(n_chunks):
        for v in range(ROW_GROUPS):
            r0 = c * q + v * SUBLANES
            for j in range(n_slabs):
                perm_scr[j, pl.ds(c * q + v, SUBLANES, stride=ROW_GROUPS), :] = (
                    branch[r0:r0 + SUBLANES, j * LANES:(j + 1) * LANES])
    for j in range(n_slabs):
        ls = slice(j * LANES, (j + 1) * LANES)
        out_ref[0, :, ls] = x_ref[0, :, ls] + perm_scr[j]


def _merge(x, u, yn, mod, pool_hist, p, *, lt, lv, pos0):
    bsz, lpad, d = x.shape
    nt = lpad // lt
    d_inner = yn.shape[-1]
    d_pool = p["wp"].shape[1]
    assert lv >= POOL_HIST and d_pool == len(POOL_WINDOWS) * GROUP_COLS
    weights = [p["gpost_mix"], p["wp"], p["wg"], p["wssd"], p["wpool"], p["pscale"], p["wo"]]
    in_specs = [
        pl.BlockSpec((1, lt, d), lambda b, t: (b, t, 0)),
        pl.BlockSpec((1, lt, d), lambda b, t: (b, t, 0)),
        pl.BlockSpec((1, lt, d_inner), lambda b, t: (b, t, 0)),
        pl.BlockSpec((1, 6, d), lambda b, t: (b, 0, 0)),
        pl.BlockSpec((1, POOL_CARRY, d_pool), lambda b, t: (b, 0, 0)),
    ] + [pl.BlockSpec(memory_space=pltpu.VMEM) for _ in weights]
    out_shape = (
        jax.ShapeDtypeStruct((bsz, lpad, d), F32),
        jax.ShapeDtypeStruct((bsz, POOL_HIST, d_pool), F32),
    )
    out_specs = (
        pl.BlockSpec((1, lt, d), lambda b, t: (b, t, 0)),
        pl.BlockSpec((1, POOL_HIST, d_pool), lambda b, t: (b, 0, 0)),
    )
    scratch = [
        pltpu.VMEM((d // LANES, lt, LANES), F32),
        pltpu.VMEM((POOL_CARRY, d_pool), F32),
        pltpu.VMEM((lt, d_pool), F32),
    ]
    kern = functools.partial(_merge_kernel, lt=lt, lv=lv, pos0=pos0)
    return pl.pallas_call(
        kern, out_shape=out_shape, grid=(bsz, nt), in_specs=in_specs, out_specs=out_specs,
        scratch_shapes=scratch,
        compiler_params=pltpu.CompilerParams(dimension_semantics=("arbitrary", "arbitrary"),
                                             vmem_limit_bytes=VMEM_LIMIT),
        name="merge",
    )(x, u, yn, mod, pool_hist, *weights)


def _mlp_kernel(x_ref, mod_ref, gpre_ref, gpost_ref, wup_ref, wdown_ref, out_ref, v_scr):
    x = x_ref[0]
    mod = mod_ref[0]
    v_scr[...] = (_rms(x, gpre_ref[...]) * (1.0 + mod[4:5]) + mod[3:4]).astype(BF16)
    slab = 512
    acc = None
    for c0 in range(0, wup_ref.shape[1], slab):
        h = jnp.square(jnp.maximum(_dot(v_scr[...], wup_ref[:, c0:c0 + slab]), 0.0)).astype(BF16)
        part = _dot(h, wdown_ref[c0:c0 + slab, :])
        acc = part if acc is None else acc + part
    out_ref[0] = x + mod[5:6] * _rms(acc, gpost_ref[...])


def _mlp(x, mod, p, *, tm):
    bsz, lpad, d = x.shape
    nt = lpad // tm
    weights = [p["gpre_mlp"], p["gpost_mlp"], p["wup"], p["wdown"]]
    return pl.pallas_call(
        _mlp_kernel,
        out_shape=jax.ShapeDtypeStruct((bsz, lpad, d), F32),
        grid=(bsz, nt),
        in_specs=[pl.BlockSpec((1, tm, d), lambda b, t: (b, t, 0)),
                  pl.BlockSpec((1, 6, d), lambda b, t: (b, 0, 0))]
                 + [pl.BlockSpec(memory_space=pltpu.VMEM) for _ in weights],
        out_specs=pl.BlockSpec((1, tm, d), lambda b, t: (b, t, 0)),
        scratch_shapes=[pltpu.VMEM((tm, d), BF16)],
        compiler_params=pltpu.CompilerParams(dimension_semantics=("arbitrary", "arbitrary"),
                                             vmem_limit_bytes=VMEM_LIMIT),
        name="mlp",
    )(x, mod, *weights)


def _prep_weights(g_pre_mix, g_post_mix, g_pre_mlp, g_post_mlp, w_in, conv_w, conv_b, dt_bias, a_log,
                  d_skip, g_ssd_norm, w_ssd_out, w_pool_group, pool_scale, w_o, w_up, w_down):
    d_inner = w_ssd_out.shape[0]
    conv_dim = conv_w.shape[1]
    n_heads = dt_bias.shape[0]
    d_pool = pool_scale.shape[0]
    s_z, s_xbc = d_inner, d_inner + conv_dim
    s_dt, s_pool = s_xbc + n_heads, s_xbc + n_heads + d_pool
    pad = LANES - n_heads
    head_of_col = jnp.arange(d_inner) // SSD_HEAD_DIM
    e1 = (jnp.arange(LANES)[:, None] == head_of_col[None, :]).astype(BF16)
    row = lambda a: a.reshape(1, -1).astype(F32)
    return dict(
        gpre=row(g_pre_mix), gpost_mix=row(g_post_mix), gpre_mlp=row(g_pre_mlp), gpost_mlp=row(g_post_mlp),
        wz=w_in[:, :s_z].astype(BF16), wxbc=w_in[:, s_z:s_xbc].astype(BF16),
        wdt=jnp.pad(w_in[:, s_xbc:s_dt], ((0, 0), (0, pad))).astype(BF16),
        wp=w_in[:, s_dt:s_pool].astype(BF16), wg=w_in[:, s_pool:].astype(BF16),
        convw=conv_w.astype(F32), convb=row(conv_b),
        dtb=jnp.pad(row(dt_bias), ((0, 0), (0, pad))), alog=jnp.pad(row(a_log), ((0, 0), (0, pad))),
        dskip=jnp.repeat(row(d_skip), SSD_HEAD_DIM, axis=1), gssd=row(g_ssd_norm),
        e2=jnp.concatenate([e1, e1], axis=0),
        wssd=w_ssd_out.astype(BF16), wpool=w_pool_group.astype(BF16), pscale=row(pool_scale),
        wo=w_o.astype(BF16), wup=w_up.astype(BF16), wdown=w_down.astype(BF16),
    )


def _history_rows(hist):
    bsz, n, c = hist.shape
    out = jnp.zeros((bsz, n, SUBLANES, c), hist.dtype).at[:, :, SUBLANES - 1, :].set(hist)
    return out.reshape(bsz, n * SUBLANES, c)


def _layer(x, mod, conv_hist, pool_hist, h0, p, *, pos0):
    bsz, seq, d = x.shape
    heads, hdim, nstate = h0.shape[1:]
    if seq % MIX_TILE == 0:
        lt, lpad = MIX_TILE, seq
    else:
        lt = lpad = -(-seq // SSD_CHUNK) * SSD_CHUNK
    lv = seq - (lpad - lt)
    xp = x if lpad == seq else jnp.pad(x, ((0, 0), (0, lpad - seq), (0, 0)))
    u, yn, h_new, conv_state = _ssd(xp, mod, _history_rows(conv_hist), h0.reshape(bsz, heads * hdim, nstate), p,
                                    lt=lt, lv=lv)
    x1, pool_state = _merge(xp, u, yn, mod, _history_rows(pool_hist), p, lt=lt, lv=lv, pos0=pos0)
    tm = MLP_TILE if lpad % MLP_TILE == 0 else lpad
    y = _mlp(x1, mod, p, tm=tm)
    if lpad != seq:
        y = y[:, :seq]
    return y, h_new.reshape(bsz, heads, hdim, nstate), conv_state, pool_state


def kernel(x_prompt, x_sample, state_ssm, state_conv, state_pool, c_prompt, c_sample, w_ada, b_ada, g_pre_mix, g_post_mix, g_pre_mlp, g_post_mlp, w_in, conv_w, conv_b, dt_bias, a_log, d_skip, g_ssd_norm, w_ssd_out, w_pool_group, pool_scale, w_o, w_up, w_down):
    yp, ys = x_prompt, x_sample
    bp, d = x_prompt.shape[0], x_prompt.shape[2]
    bs = x_sample.shape[0]
    depth = w_in.shape[0]
    outs = [[] for _ in range(6)]
    for l in range(depth):
        p = _prep_weights(g_pre_mix[l], g_post_mix[l], g_pre_mlp[l], g_post_mlp[l], w_in[l], conv_w[l],
                          conv_b[l], dt_bias[l], a_log[l], d_skip[l], g_ssd_norm[l], w_ssd_out[l],
                          w_pool_group[l], pool_scale[l], w_o[l], w_up[l], w_down[l])
        mod = _adaln(jnp.concatenate([c_prompt, c_sample], axis=0), w_ada[l], b_ada[l])
        mod_p = mod[:bp].reshape(bp, 6, d)
        mod_s = mod[bp:].reshape(bs, 6, d)
        zc = jnp.zeros((bp,) + state_conv.shape[2:], yp.dtype)
        zp = jnp.zeros((bp,) + state_pool.shape[2:], yp.dtype)
        zh = jnp.zeros((bp,) + state_ssm.shape[2:], yp.dtype)
        yp, h_p, c_p, p_p = _layer(yp, mod_p, zc, zp, zh, p, pos0=0)
        ys, h_s, c_s, p_s = _layer(ys, mod_s, state_conv[l], state_pool[l], state_ssm[l], p, pos0=PAST_LEN)
        for lst, val in zip(outs, (h_p, c_p, p_p, h_s, c_s, p_s)):
            lst.append(val)
    return (yp, ys) + tuple(jnp.stack(o) for o in outs)
```

```python
import functools

import jax
import jax.numpy as jnp
from jax import lax
from jax.experimental import pallas as pl
from jax.experimental.pallas import tpu as pltpu

F32 = jnp.float32
BF16 = jnp.bfloat16

EPS = 1e-6
LOG2E = 1.4426950408889634
PAST_LEN = 2048
POOL_WINDOWS = (2, 4, 8, 16)
POOL_HIST = max(POOL_WINDOWS) - 1
CONV_WIDTH = 4
SSD_HEAD_DIM = 64
SSD_STATE = 128
LANES = 128
SUBLANES = 8
SSD_CHUNK = 128
ROW_GROUPS = SSD_CHUNK // SUBLANES
GROUP_COLS = 256
HEADS_PER_GROUP = GROUP_COLS // SSD_HEAD_DIM
PROJ_COLS = 512
CONV_COLS = SSD_STATE
CONV_CARRY = (CONV_WIDTH - 1) * SUBLANES
POOL_CARRY = POOL_HIST * SUBLANES
MIX_TILE = 512
MLP_TILE = 512
VMEM_LIMIT = 60 * 1024 * 1024


def _row_of_token(tok):
    return (tok % ROW_GROUPS) * SUBLANES + tok // ROW_GROUPS


def _tile_row(tok):
    return (tok // SSD_CHUNK) * SSD_CHUNK + _row_of_token(tok % SSD_CHUNK)


def _tokens(shape, axis):
    r = lax.broadcasted_iota(jnp.int32, shape, axis)
    return (r // SSD_CHUNK) * SSD_CHUNK + (r % SUBLANES) * ROW_GROUPS + (r % SSD_CHUNK) // SUBLANES


def _dot(a, b):
    return jnp.dot(a, b, preferred_element_type=F32)


def _sigmoid(v):
    return 1.0 / (1.0 + jnp.exp2(v * (-LOG2E)))


def _silu(v):
    return v * _sigmoid(v)


def _rms(v, g):
    return v * lax.rsqrt(jnp.mean(v * v, axis=-1, keepdims=True) + EPS) * g


def _shifted_groups(prev_tail, cur, n_back):
    first = lax.broadcasted_iota(jnp.int32, (SUBLANES, cur.shape[1]), 0) == 0
    tail = cur[SSD_CHUNK - n_back * SUBLANES:]
    parts = []
    for j in range(n_back):
        rs = slice(j * SUBLANES, (j + 1) * SUBLANES)
        parts.append(jnp.where(first, pltpu.roll(prev_tail[rs], 1, axis=0), pltpu.roll(tail[rs], 1, axis=0)))
    return parts


def _adaln_kernel(c_ref, w_ref, b_ref, o_ref):
    c = c_ref[...]
    o_ref[...] = jnp.dot(_silu(c), w_ref[...], preferred_element_type=F32,
                         precision=lax.Precision.HIGHEST) + b_ref[...]


def _adaln(c, w_ada, b_ada):
    rows, d = c.shape
    n = w_ada.shape[1]
    bn = d
    return pl.pallas_call(
        _adaln_kernel,
        out_shape=jax.ShapeDtypeStruct((rows, n), F32),
        grid=(n // bn,),
        in_specs=[pl.BlockSpec((rows, d), lambda j: (0, 0)),
                  pl.BlockSpec((d, bn), lambda j: (0, j)),
                  pl.BlockSpec((1, bn), lambda j: (0, j))],
        out_specs=pl.BlockSpec((rows, bn), lambda j: (0, j)),
        compiler_params=pltpu.CompilerParams(dimension_semantics=("arbitrary",)),
        name="adaln",
    )(c, w_ada, b_ada.reshape(1, n))


def _ssd_kernel(x_ref, mod_ref, convh_ref, h0_ref,
                gpre_ref, wz_ref, wxbc_ref, wdt_ref, convw_ref, convb_ref, dtb_ref, alog_ref, dskip_ref,
                gssd_ref, e2_ref,
                u_ref, yn_ref, hout_ref, convo_ref,
                perm_scr, convc_scr, xr_scr, bmt_scr, cm_scr, acs_scr, acst_scr,
                ea_scr, dte_scr, y_scr, st_scr, cb_scr, rcat_scr, mcat_scr,
                *, lt, lv, d_inner, n_groups):
    q = SSD_CHUNK
    n_chunks = lt // q
    t = pl.program_id(1)
    last_t = pl.num_programs(1) - 1
    d_bc = n_groups * SSD_STATE
    d_model = x_ref.shape[-1]
    n_slabs = d_model // LANES

    @pl.when(t == 0)
    def _init():
        convc_scr[...] = convh_ref[0]
        rcat_scr[...] = jnp.zeros(rcat_scr.shape, BF16)
        for g in range(n_groups):
            cs = slice(g * GROUP_COLS, (g + 1) * GROUP_COLS)
            st_scr[:, cs] = h0_ref[0, cs, :].T

    mod = mod_ref[0]
    u = _rms(x_ref[0], gpre_ref[...]) * (1.0 + mod[1:2]) + mod[0:1]
    for c in range(n_chunks):
        for i in range(ROW_GROUPS):
            start = c * q + (i // 2 if i % 2 == 0 else q // 2 + i // 2)
            r0 = c * q + i * SUBLANES
            for j in range(n_slabs):
                perm_scr[j, pl.ds(start, SUBLANES, stride=SUBLANES), :] = (
                    u[r0:r0 + SUBLANES, j * LANES:(j + 1) * LANES])
    for j in range(n_slabs):
        u_ref[0, :, j * LANES:(j + 1) * LANES] = perm_scr[j].astype(BF16)

    lane = lax.broadcasted_iota(jnp.int32, (1, LANES), 1)
    n_heads = d_inner // SSD_HEAD_DIM
    dt = jax.nn.softplus(_dot(u_ref[0], wdt_ref[...]) + dtb_ref[...])
    if lv < lt:
        dt = jnp.where(_tokens((lt, LANES), 0) < lv, dt, 0.0)
    a_row = jnp.where(lane < n_heads, -jnp.exp(alog_ref[...]), 0.0)
    da = dt * a_row
    causal = _tokens((q, q), 0) >= _tokens((q, q), 1)
    tri = causal.astype(F32)
    ea_parts, dte_parts = [], []
    for c in range(n_chunks):
        acs = jnp.dot(tri, da[c * q:(c + 1) * q], preferred_element_type=F32,
                      precision=lax.Precision.HIGHEST)
        acs_scr[c * q:(c + 1) * q, :] = acs * LOG2E
        acst_scr[c] = (acs * LOG2E).T
        ea_parts.append(jnp.exp(acs))
        dte_parts.append(jnp.exp(acs[q - 1:q, :] - acs))

    def split_hi_lo(v):
        hi = v.astype(BF16)
        lo = (v - hi.astype(F32)).astype(BF16)
        return jnp.concatenate([hi, lo], axis=1)

    dt2 = split_hi_lo(dt)
    ea2 = split_hi_lo(jnp.concatenate(ea_parts, axis=0))
    dte2 = split_hi_lo(jnp.concatenate(dte_parts, axis=0))
    for c0 in range(0, d_inner, PROJ_COLS):
        cs = slice(c0, c0 + PROJ_COLS)
        ea_scr[:, cs] = _dot(ea2, e2_ref[:, cs])
        dte_scr[:, cs] = _dot(dte2, e2_ref[:, cs])

    for p0 in range(0, d_inner + 2 * d_bc, PROJ_COLS):
        ps = slice(p0, p0 + PROJ_COLS)
        res_wide = _dot(u_ref[0], wxbc_ref[:, ps])
        dt_wide = _dot(dt2, e2_ref[:, ps]) if p0 < d_inner else None
        for h0 in range(0, PROJ_COLS, CONV_COLS):
            c0 = p0 + h0
            cs = slice(c0, c0 + CONV_COLS)
            res = res_wide[:, h0:h0 + CONV_COLS]
            prev_tail = convc_scr[:, cs]
            for c in range(n_chunks):
                rows = slice(c * q, (c + 1) * q)
                cur = res[rows]
                ext = jnp.concatenate(_shifted_groups(prev_tail, cur, CONV_WIDTH - 1) + [cur], axis=0)
                acc = convb_ref[:, cs] + ext[CONV_CARRY:] * convw_ref[CONV_WIDTH - 1:CONV_WIDTH, cs]
                for k in range(1, CONV_WIDTH):
                    lo = CONV_CARRY - k * SUBLANES
                    acc = acc + ext[lo:lo + q] * convw_ref[CONV_WIDTH - 1 - k:CONV_WIDTH - k, cs]
                v = _silu(acc)
                if c0 < d_inner:
                    y_scr[rows, cs] = v * dskip_ref[:, cs]
                    xr_scr[rows, cs] = v * dt_wide[rows, h0:h0 + CONV_COLS]
                elif c0 < d_inner + d_bc:
                    b0 = c0 - d_inner
                    bmt_scr[c * n_groups + b0 // SSD_STATE] = v.T.astype(BF16)
                else:
                    b0 = c0 - d_inner - d_bc
                    cm_scr[rows, b0:b0 + CONV_COLS] = v.astype(BF16)
                prev_tail = cur[q - CONV_CARRY:]
            convc_scr[:, cs] = prev_tail
            for j in range(CONV_WIDTH - 1):
                r = _tile_row(lv - (CONV_WIDTH - 1) + j)
                convo_ref[0, j:j + 1, cs] = res[r:r + 1]

    def chunk_body(c, carry):
        r0 = pl.multiple_of(c * q, q)
        crow = pl.ds(r0, q)
        for g in range(n_groups):
            gs = slice(g * GROUP_COLS, (g + 1) * GROUP_COLS)
            ns = slice(g * SSD_STATE, (g + 1) * SSD_STATE)
            cb_scr[g] = _dot(cm_scr[crow, ns], bmt_scr[c * n_groups + g])
            xr_bf = xr_scr[crow, gs].astype(BF16)
            for j in range(HEADS_PER_GROUP):
                ls = slice(j * SSD_HEAD_DIM, (j + 1) * SSD_HEAD_DIM)
                rcat_scr[g, j * q:(j + 1) * q, ls] = xr_bf[:, ls]
        acs = acs_scr[crow, :]
        acst = acst_scr[c]
        for g in range(n_groups):
            for j in range(HEADS_PER_GROUP):
                h = g * HEADS_PER_GROUP + j
                seg = acs[:, h:h + 1] - acst[h:h + 1, :]
                dec = jnp.exp2(jnp.where(causal, seg, -jnp.inf))
                mcat_scr[g, :, j * q:(j + 1) * q] = (cb_scr[g] * dec).astype(BF16)
        for g in range(n_groups):
            gs = slice(g * GROUP_COLS, (g + 1) * GROUP_COLS)
            ns = slice(g * SSD_STATE, (g + 1) * SSD_STATE)
            y_diag = _dot(mcat_scr[g], rcat_scr[g])
            st = st_scr[:, gs]
            y_off = _dot(cm_scr[crow, ns], st.astype(BF16)) * ea_scr[crow, gs]
            y_scr[crow, gs] = y_scr[crow, gs] + y_diag + y_off
            xrd = (xr_scr[crow, gs] * dte_scr[crow, gs]).astype(BF16)
            cd = ea_scr[pl.ds(r0 + q - 1, 1), gs]
            st_scr[:, gs] = st * cd + _dot(bmt_scr[c * n_groups + g], xrd)
        return carry

    lax.fori_loop(0, n_chunks, chunk_body, 0)

    @pl.when(t == last_t)
    def _ssm_state():
        for g in range(n_groups):
            cs = slice(g * GROUP_COLS, (g + 1) * GROUP_COLS)
            hout_ref[0, cs, :] = st_scr[:, cs].T

    for p0 in range(0, d_inner, PROJ_COLS):
        z_wide = _dot(u_ref[0], wz_ref[:, p0:p0 + PROJ_COLS])
        for h0 in range(0, PROJ_COLS, GROUP_COLS):
            gs = slice(p0 + h0, p0 + h0 + GROUP_COLS)
            yn_ref[0, :, gs] = _rms(y_scr[:, gs] * _silu(z_wide[:, h0:h0 + GROUP_COLS]),
                                    gssd_ref[:, gs]).astype(BF16)


def _ssd(x, mod, conv_hist, h0, p, *, lt, lv):
    bsz, lpad, d = x.shape
    nt = lpad // lt
    d_inner = p["wz"].shape[1]
    conv_dim = p["wxbc"].shape[1]
    n_groups = (conv_dim - d_inner) // (2 * SSD_STATE)
    n_chunks = lt // SSD_CHUNK
    hp = h0.shape[1]
    assert lv >= CONV_WIDTH - 1 and lt % SSD_CHUNK == 0 and d_inner == n_groups * GROUP_COLS

    weights = [p["gpre"], p["wz"], p["wxbc"], p["wdt"], p["convw"], p["convb"], p["dtb"], p["alog"],
               p["dskip"], p["gssd"], p["e2"]]
    in_specs = [
        pl.BlockSpec((1, lt, d), lambda b, t: (b, t, 0)),
        pl.BlockSpec((1, 6, d), lambda b, t: (b, 0, 0)),
        pl.BlockSpec((1, CONV_CARRY, conv_dim), lambda b, t: (b, 0, 0)),
        pl.BlockSpec((1, hp, SSD_STATE), lambda b, t: (b, 0, 0)),
    ] + [pl.BlockSpec(memory_space=pltpu.VMEM) for _ in weights]
    out_shape = (
        jax.ShapeDtypeStruct((bsz, lpad, d), BF16),
        jax.ShapeDtypeStruct((bsz, lpad, d_inner), BF16),
        jax.ShapeDtypeStruct((bsz, hp, SSD_STATE), F32),
        jax.ShapeDtypeStruct((bsz, CONV_WIDTH - 1, conv_dim), F32),
    )
    out_specs = (
        pl.BlockSpec((1, lt, d), lambda b, t: (b, t, 0)),
        pl.BlockSpec((1, lt, d_inner), lambda b, t: (b, t, 0)),
        pl.BlockSpec((1, hp, SSD_STATE), lambda b, t: (b, 0, 0)),
        pl.BlockSpec((1, CONV_WIDTH - 1, conv_dim), lambda b, t: (b, 0, 0)),
    )
    scratch = [
        pltpu.VMEM((d // LANES, lt, LANES), F32),
        pltpu.VMEM((CONV_CARRY, conv_dim), F32),
        pltpu.VMEM((lt, d_inner), F32),
        pltpu.VMEM((n_chunks * n_groups, SSD_STATE, SSD_CHUNK), BF16),
        pltpu.VMEM((lt, n_groups * SSD_STATE), BF16),
        pltpu.VMEM((lt, LANES), F32),
        pltpu.VMEM((n_chunks, LANES, SSD_CHUNK), F32),
        pltpu.VMEM((lt, d_inner), F32),
        pltpu.VMEM((lt, d_inner), F32),
        pltpu.VMEM((lt, d_inner), F32),
        pltpu.VMEM((SSD_STATE, hp), F32),
        pltpu.VMEM((n_groups, SSD_CHUNK, SSD_CHUNK), F32),
        pltpu.VMEM((n_groups, HEADS_PER_GROUP * SSD_CHUNK, GROUP_COLS), BF16),
        pltpu.VMEM((n_groups, SSD_CHUNK, HEADS_PER_GROUP * SSD_CHUNK), BF16),
    ]
    kern = functools.partial(_ssd_kernel, lt=lt, lv=lv, d_inner=d_inner, n_groups=n_groups)
    return pl.pallas_call(
        kern, out_shape=out_shape, grid=(bsz, nt), in_specs=in_specs, out_specs=out_specs,
        scratch_shapes=scratch,
        compiler_params=pltpu.CompilerParams(dimension_semantics=("arbitrary", "arbitrary"),
                                             vmem_limit_bytes=VMEM_LIMIT),
        name="ssd",
    )(x, mod, conv_hist, h0, *weights)


def _merge_kernel(x_ref, u_ref, yn_ref, mod_ref, poolh_ref,
                  gpost_ref, wp_ref, wg_ref, wssd_ref, wpool_ref, pscale_ref, wo_ref,
                  out_ref, poolo_ref,
                  perm_scr, poolc_scr, ypool_scr,
                  *, lt, lv, pos0):
    q = SSD_CHUNK
    n_chunks = lt // q
    t = pl.program_id(1)
    d_model = x_ref.shape[-1]
    n_slabs = d_model // LANES

    @pl.when(t == 0)
    def _init():
        poolc_scr[...] = poolh_ref[0]

    p_all = _dot(u_ref[0], wp_ref[...])
    for gi, w in enumerate(POOL_WINDOWS):
        gs = slice(gi * GROUP_COLS, (gi + 1) * GROUP_COLS)
        res = p_all[:, gs]
        prev_tail = poolc_scr[:, gs]
        for c in range(n_chunks):
            rows = slice(c * q, (c + 1) * q)
            cur = res[rows]
            back = _shifted_groups(prev_tail[POOL_CARRY - (w - 1) * SUBLANES:], cur, w - 1)
            win = jnp.concatenate(back + [cur], axis=0)
            span = 1
            while span < w:
                win = win[span * SUBLANES:] + win[:win.shape[0] - span * SUBLANES]
                span *= 2
            if pos0 >= POOL_HIST or c > 0:
                pm = win * (1.0 / w) - cur
            else:
                cnt = jnp.minimum(_tokens((q, GROUP_COLS), 0) + (pos0 + 1), w).astype(F32)
                inv = jnp.where(t == 0, 1.0 / cnt, 1.0 / w)
                pm = win * inv - cur
            ypool_scr[rows, gs] = _dot(pm.astype(BF16), wpool_ref[gi]) * pscale_ref[:, gs]
            prev_tail = cur[SUBLANES:]
        poolc_scr[:, gs] = prev_tail
        for j in range(POOL_HIST):
            r = _tile_row(lv - POOL_HIST + j)
            poolo_ref[0, j:j + 1, gs] = res[r:r + 1]

    y_ssd = _dot(yn_ref[0], wssd_ref[...])
    gate_ssd = _sigmoid(_dot(u_ref[0], wg_ref[:, 0:d_model]))
    gate_pool = _sigmoid(_dot(u_ref[0], wg_ref[:, d_model:2 * d_model]))
    merged = (gate_ssd * y_ssd + gate_pool * ypool_scr[...]).astype(BF16)
    mix = _dot(merged, wo_ref[...])
    branch = mod_ref[0][2:3] * _rms(mix, gpost_ref[...])
    for c in range(n_chunks):
        for v in range(ROW_GROUPS):
            r0 = c * q + v * SUBLANES
            for j in range(n_slabs):
                perm_scr[j, pl.ds(c * q + v, SUBLANES, stride=ROW_GROUPS), :] = (
                    branch[r0:r0 + SUBLANES, j * LANES:(j + 1) * LANES])
    for j in range(n_slabs):
        ls = slice(j * LANES, (j + 1) * LANES)
        out_ref[0, :, ls] = x_ref[0, :, ls] + perm_scr[j]


def _merge(x, u, yn, mod, pool_hist, p, *, lt, lv, pos0):
    bsz, lpad, d = x.shape
    nt = lpad // lt
    d_inner = yn.shape[-1]
    d_pool = p["wp"].shape[1]
    assert lv >= POOL_HIST and d_pool == len(POOL_WINDOWS) * GROUP_COLS
    weights = [p["gpost_mix"], p["wp"], p["wg"], p["wssd"], p["wpool"], p["pscale"], p["wo"]]
    in_specs = [
        pl.BlockSpec((1, lt, d), lambda b, t: (b, t, 0)),
        pl.BlockSpec((1, lt, d), lambda b, t: (b, t, 0)),
        pl.BlockSpec((1, lt, d_inner), lambda b, t: (b, t, 0)),
        pl.BlockSpec((1, 6, d), lambda b, t: (b, 0, 0)),
        pl.BlockSpec((1, POOL_CARRY, d_pool), lambda b, t: (b, 0, 0)),
    ] + [pl.BlockSpec(memory_space=pltpu.VMEM) for _ in weights]
    out_shape = (
        jax.ShapeDtypeStruct((bsz, lpad, d), F32),
        jax.ShapeDtypeStruct((bsz, POOL_HIST, d_pool), F32),
    )
    out_specs = (
        pl.BlockSpec((1, lt, d), lambda b, t: (b, t, 0)),
        pl.BlockSpec((1, POOL_HIST, d_pool), lambda b, t: (b, 0, 0)),
    )
    scratch = [
        pltpu.VMEM((d // LANES, lt, LANES), F32),
        pltpu.VMEM((POOL_CARRY, d_pool), F32),
        pltpu.VMEM((lt, d_pool), F32),
    ]
    kern = functools.partial(_merge_kernel, lt=lt, lv=lv, pos0=pos0)
    return pl.pallas_call(
        kern, out_shape=out_shape, grid=(bsz, nt), in_specs=in_specs, out_specs=out_specs,
        scratch_shapes=scratch,
        compiler_params=pltpu.CompilerParams(dimension_semantics=("arbitrary", "arbitrary"),
                                             vmem_limit_bytes=VMEM_LIMIT),
        name="merge",
    )(x, u, yn, mod, pool_hist, *weights)


def _mlp_kernel(x_ref, mod_ref, gpre_ref, gpost_ref, wup_ref, wdown_ref, out_ref, v_scr):
    x = x_ref[0]
    mod = mod_ref[0]
    v_scr[...] = (_rms(x, gpre_ref[...]) * (1.0 + mod[4:5]) + mod[3:4]).astype(BF16)
    slab = 512
    acc = None
    for c0 in range(0, wup_ref.shape[1], slab):
        h = jnp.square(jnp.maximum(_dot(v_scr[...], wup_ref[:, c0:c0 + slab]), 0.0)).astype(BF16)
        part = _dot(h, wdown_ref[c0:c0 + slab, :])
        acc = part if acc is None else acc + part
    out_ref[0] = x + mod[5:6] * _rms(acc, gpost_ref[...])


def _mlp(x, mod, p, *, tm):
    bsz, lpad, d = x.shape
    nt = lpad // tm
    weights = [p["gpre_mlp"], p["gpost_mlp"], p["wup"], p["wdown"]]
    return pl.pallas_call(
        _mlp_kernel,
        out_shape=jax.ShapeDtypeStruct((bsz, lpad, d), F32),
        grid=(bsz, nt),
        in_specs=[pl.BlockSpec((1, tm, d), lambda b, t: (b, t, 0)),
                  pl.BlockSpec((1, 6, d), lambda b, t: (b, 0, 0))]
                 + [pl.BlockSpec(memory_space=pltpu.VMEM) for _ in weights],
        out_specs=pl.BlockSpec((1, tm, d), lambda b, t: (b, t, 0)),
        scratch_shapes=[pltpu.VMEM((tm, d), BF16)],
        compiler_params=pltpu.CompilerParams(dimension_semantics=("arbitrary", "arbitrary"),
                                             vmem_limit_bytes=VMEM_LIMIT),
        name="mlp",
    )(x, mod, *weights)


def _prep_weights(g_pre_mix, g_post_mix, g_pre_mlp, g_post_mlp, w_in, conv_w, conv_b, dt_bias, a_log,
                  d_skip, g_ssd_norm, w_ssd_out, w_pool_group, pool_scale, w_o, w_up, w_down):
    d_inner = w_ssd_out.shape[0]
    conv_dim = conv_w.shape[1]
    n_heads = dt_bias.shape[0]
    d_pool = pool_scale.shape[0]
    s_z, s_xbc = d_inner, d_inner + conv_dim
    s_dt, s_pool = s_xbc + n_heads, s_xbc + n_heads + d_pool
    pad = LANES - n_heads
    head_of_col = jnp.arange(d_inner) // SSD_HEAD_DIM
    e1 = (jnp.arange(LANES)[:, None] == head_of_col[None, :]).astype(BF16)
    row = lambda a: a.reshape(1, -1).astype(F32)
    return dict(
        gpre=row(g_pre_mix), gpost_mix=row(g_post_mix), gpre_mlp=row(g_pre_mlp), gpost_mlp=row(g_post_mlp),
        wz=w_in[:, :s_z].astype(BF16), wxbc=w_in[:, s_z:s_xbc].astype(BF16),
        wdt=jnp.pad(w_in[:, s_xbc:s_dt], ((0, 0), (0, pad))).astype(BF16),
        wp=w_in[:, s_dt:s_pool].astype(BF16), wg=w_in[:, s_pool:].astype(BF16),
        convw=conv_w.astype(F32), convb=row(conv_b),
        dtb=jnp.pad(row(dt_bias), ((0, 0), (0, pad))), alog=jnp.pad(row(a_log), ((0, 0), (0, pad))),
        dskip=jnp.repeat(row(d_skip), SSD_HEAD_DIM, axis=1), gssd=row(g_ssd_norm),
        e2=jnp.concatenate([e1, e1], axis=0),
        wssd=w_ssd_out.astype(BF16), wpool=w_pool_group.astype(BF16), pscale=row(pool_scale),
        wo=w_o.astype(BF16), wup=w_up.astype(BF16), wdown=w_down.astype(BF16),
    )


def _history_rows(hist):
    bsz, n, c = hist.shape
    out = jnp.zeros((bsz, n, SUBLANES, c), hist.dtype).at[:, :, SUBLANES - 1, :].set(hist)
    return out.reshape(bsz, n * SUBLANES, c)


def _layer(x, mod, conv_hist, pool_hist, h0, p, *, pos0):
    bsz, seq, d = x.shape
    heads, hdim, nstate = h0.shape[1:]
    if seq % MIX_TILE == 0:
        lt, lpad = MIX_TILE, seq
    else:
        lt = lpad = -(-seq // SSD_CHUNK) * SSD_CHUNK
    lv = seq - (lpad - lt)
    xp = x if lpad == seq else jnp.pad(x, ((0, 0), (0, lpad - seq), (0, 0)))
    u, yn, h_new, conv_state = _ssd(xp, mod, _history_rows(conv_hist), h0.reshape(bsz, heads * hdim, nstate), p,
                                    lt=lt, lv=lv)
    x1, pool_state = _merge(xp, u, yn, mod, _history_rows(pool_hist), p, lt=lt, lv=lv, pos0=pos0)
    tm = MLP_TILE if lpad % MLP_TILE == 0 else lpad
    y = _mlp(x1, mod, p, tm=tm)
    if lpad != seq:
        y = y[:, :seq]
    return y, h_new.reshape(bsz, heads, hdim, nstate), conv_state, pool_state


def kernel(x_prompt, x_sample, state_ssm, state_conv, state_pool, c_prompt, c_sample, w_ada, b_ada, g_pre_mix, g_post_mix, g_pre_mlp, g_post_mlp, w_in, conv_w, conv_b, dt_bias, a_log, d_skip, g_ssd_norm, w_ssd_out, w_pool_group, pool_scale, w_o, w_up, w_down):
    yp, ys = x_prompt, x_sample
    bp, d = x_prompt.shape[0], x_prompt.shape[2]
    bs = x_sample.shape[0]
    depth = w_in.shape[0]
    outs = [[] for _ in range(6)]
    for l in range(depth):
        p = _prep_weights(g_pre_mix[l], g_post_mix[l], g_pre_mlp[l], g_post_mlp[l], w_in[l], conv_w[l],
                          conv_b[l], dt_bias[l], a_log[l], d_skip[l], g_ssd_norm[l], w_ssd_out[l],
                          w_pool_group[l], pool_scale[l], w_o[l], w_up[l], w_down[l])
        mod = _adaln(jnp.concatenate([c_prompt, c_sample], axis=0), w_ada[l], b_ada[l])
        mod_p = mod[:bp].reshape(bp, 6, d)
        mod_s = mod[bp:].reshape(bs, 6, d)
        zc = jnp.zeros((bp,) + state_conv.shape[2:], yp.dtype)
        zp = jnp.zeros((bp,) + state_pool.shape[2:], yp.dtype)
        zh = jnp.zeros((bp,) + state_ssm.shape[2:], yp.dtype)
        yp, h_p, c_p, p_p = _layer(yp, mod_p, zc, zp, zh, p, pos0=0)
        ys, h_s, c_s, p_s = _layer(ys, mod_s, state_conv[l], state_pool[l], state_ssm[l], p, pos0=PAST_LEN)
        for lst, val in zip(outs, (h_p, c_p, p_p, h_s, c_s, p_s)):
            lst.append(val)
    return (yp, ys) + tuple(jnp.stack(o) for o in outs)
```

```python
import functools

import jax
import jax.numpy as jnp
from jax import lax
from jax.experimental import pallas as pl
from jax.experimental.pallas import tpu as pltpu

F32 = jnp.float32
BF16 = jnp.bfloat16

EPS = 1e-6
LOG2E = 1.4426950408889634
PAST_LEN = 2048
POOL_WINDOWS = (2, 4, 8, 16)
POOL_HIST = max(POOL_WINDOWS) - 1
CONV_WIDTH = 4
SSD_HEAD_DIM = 64
SSD_STATE = 128
LANES = 128
SUBLANES = 8
SSD_CHUNK = 128
ROW_GROUPS = SSD_CHUNK // SUBLANES
GROUP_COLS = 256
HEADS_PER_GROUP = GROUP_COLS // SSD_HEAD_DIM
PROJ_COLS = 512
CONV_COLS = SSD_STATE
CONV_CARRY = (CONV_WIDTH - 1) * SUBLANES
POOL_CARRY = POOL_HIST * SUBLANES
MIX_TILE = 512
MLP_TILE = 512
VMEM_LIMIT = 60 * 1024 * 1024


def _row_of_token(tok):
    return (tok % ROW_GROUPS) * SUBLANES + tok // ROW_GROUPS


def _tile_row(tok):
    return (tok // SSD_CHUNK) * SSD_CHUNK + _row_of_token(tok % SSD_CHUNK)


def _tokens(shape, axis):
    r = lax.broadcasted_iota(jnp.int32, shape, axis)
    return (r // SSD_CHUNK) * SSD_CHUNK + (r % SUBLANES) * ROW_GROUPS + (r % SSD_CHUNK) // SUBLANES


def _dot(a, b):
    return jnp.dot(a, b, preferred_element_type=F32)


def _sigmoid(v):
    return 1.0 / (1.0 + jnp.exp2(v * (-LOG2E)))


def _silu(v):
    return v * _sigmoid(v)


def _rms(v, g):
    return v * lax.rsqrt(jnp.mean(v * v, axis=-1, keepdims=True) + EPS) * g


def _shifted_groups(prev_tail, cur, n_back, packed):
    if packed:
        return [prev_tail[j * SUBLANES:(j + 1) * SUBLANES] for j in range(n_back)]
    first = lax.broadcasted_iota(jnp.int32, (SUBLANES, cur.shape[1]), 0) == 0
    tail = cur[SSD_CHUNK - n_back * SUBLANES:]
    parts = []
    for j in range(n_back):
        rs = slice(j * SUBLANES, (j + 1) * SUBLANES)
        parts.append(jnp.where(first, pltpu.roll(prev_tail[rs], 1, axis=0), pltpu.roll(tail[rs], 1, axis=0)))
    return parts


def _adaln_kernel(c_ref, w_ref, b_ref, o_ref):
    c = c_ref[...]
    o_ref[...] = jnp.dot(_silu(c), w_ref[...], preferred_element_type=F32,
                         precision=lax.Precision.HIGHEST) + b_ref[...]


def _adaln(c, w_ada, b_ada):
    rows, d = c.shape
    n = w_ada.shape[1]
    bn = d
    return pl.pallas_call(
        _adaln_kernel,
        out_shape=jax.ShapeDtypeStruct((rows, n), F32),
        grid=(n // bn,),
        in_specs=[pl.BlockSpec((rows, d), lambda j: (0, 0)),
                  pl.BlockSpec((d, bn), lambda j: (0, j)),
                  pl.BlockSpec((1, bn), lambda j: (0, j))],
        out_specs=pl.BlockSpec((rows, bn), lambda j: (0, j)),
        compiler_params=pltpu.CompilerParams(dimension_semantics=("arbitrary",)),
        name="adaln",
    )(c, w_ada, b_ada.reshape(1, n))


def _ssd_kernel(x_ref, mod_ref, convh_ref, h0_ref,
                gpre_ref, wz_ref, wxbc_ref, wdt_ref, convw_ref, convb_ref, dtb_ref, alog_ref, dskip_ref,
                gssd_ref, e2_ref,
                u_ref, yn_ref, hout_ref, convo_ref,
                perm_scr, convc_scr, xr_scr, bmt_scr, cm_scr, acs_scr, acst_scr,
                ea_scr, dte_scr, y_scr, st_scr, cb_scr, rcat_scr, mcat_scr,
                *, lt, lv, d_inner, n_groups, nseq):
    q = SSD_CHUNK
    n_chunks = lt // q
    t = pl.program_id(1)
    last_t = pl.num_programs(1) - 1
    d_bc = n_groups * SSD_STATE
    d_model = x_ref.shape[-1]
    n_slabs = d_model // LANES
    packed = nseq > 1

    @pl.when(t == 0)
    def _init():
        convc_scr[...] = convh_ref[0]
        rcat_scr[...] = jnp.zeros(rcat_scr.shape, BF16)
        for s in range(nseq):
            for g in range(n_groups):
                cs = slice(g * GROUP_COLS, (g + 1) * GROUP_COLS)
                st_scr[s, :, cs] = h0_ref[s, cs, :].T

    u = _rms(x_ref[0], gpre_ref[...]) * (1.0 + mod_ref[0, 1]) + mod_ref[0, 0]
    for c in range(n_chunks):
        for i in range(ROW_GROUPS):
            start = c * q + (i // 2 if i % 2 == 0 else q // 2 + i // 2)
            r0 = c * q + i * SUBLANES
            for j in range(n_slabs):
                perm_scr[j, pl.ds(start, SUBLANES, stride=SUBLANES), :] = (
                    u[r0:r0 + SUBLANES, j * LANES:(j + 1) * LANES])
    for j in range(n_slabs):
        u_ref[0, :, j * LANES:(j + 1) * LANES] = perm_scr[j].astype(BF16)

    lane = lax.broadcasted_iota(jnp.int32, (1, LANES), 1)
    n_heads = d_inner // SSD_HEAD_DIM
    dt = jax.nn.softplus(_dot(u_ref[0], wdt_ref[...]) + dtb_ref[...])
    if lv < lt:
        dt = jnp.where(_tokens((lt, LANES), 0) < lv, dt, 0.0)
    a_row = jnp.where(lane < n_heads, -jnp.exp(alog_ref[...]), 0.0)
    da = dt * a_row
    tok_l, tok_s = _tokens((q, q), 0), _tokens((q, q), 1)
    causal = tok_l >= tok_s
    if packed:
        causal = causal & (tok_l // ROW_GROUPS == tok_s // ROW_GROUPS)
    tri = causal.astype(F32)
    ea_parts, dte_parts = [], []
    for c in range(n_chunks):
        acs = jnp.dot(tri, da[c * q:(c + 1) * q], preferred_element_type=F32,
                      precision=lax.Precision.HIGHEST)
        acs_scr[c * q:(c + 1) * q, :] = acs * LOG2E
        acst_scr[c] = (acs * LOG2E).T
        ea_parts.append(jnp.exp(acs))
        last = jnp.concatenate([acs[q - SUBLANES:]] * ROW_GROUPS, axis=0) if packed else acs[q - 1:q, :]
        dte_parts.append(jnp.exp(last - acs))

    def split_hi_lo(v):
        hi = v.astype(BF16)
        lo = (v - hi.astype(F32)).astype(BF16)
        return jnp.concatenate([hi, lo], axis=1)

    dt2 = split_hi_lo(dt)
    ea2 = split_hi_lo(jnp.concatenate(ea_parts, axis=0))
    dte2 = split_hi_lo(jnp.concatenate(dte_parts, axis=0))
    for c0 in range(0, d_inner, PROJ_COLS):
        cs = slice(c0, c0 + PROJ_COLS)
        ea_scr[:, cs] = _dot(ea2, e2_ref[:, cs])
        dte_scr[:, cs] = _dot(dte2, e2_ref[:, cs])

    for p0 in range(0, d_inner + 2 * d_bc, PROJ_COLS):
        ps = slice(p0, p0 + PROJ_COLS)
        res_wide = _dot(u_ref[0], wxbc_ref[:, ps])
        dt_wide = _dot(dt2, e2_ref[:, ps]) if p0 < d_inner else None
        for h0 in range(0, PROJ_COLS, CONV_COLS):
            c0 = p0 + h0
            cs = slice(c0, c0 + CONV_COLS)
            res = res_wide[:, h0:h0 + CONV_COLS]
            prev_tail = convc_scr[:, cs]
            for c in range(n_chunks):
                rows = slice(c * q, (c + 1) * q)
                cur = res[rows]
                ext = jnp.concatenate(_shifted_groups(prev_tail, cur, CONV_WIDTH - 1, packed) + [cur], axis=0)
                acc = convb_ref[:, cs] + ext[CONV_CARRY:] * convw_ref[CONV_WIDTH - 1:CONV_WIDTH, cs]
                for k in range(1, CONV_WIDTH):
                    lo = CONV_CARRY - k * SUBLANES
                    acc = acc + ext[lo:lo + q] * convw_ref[CONV_WIDTH - 1 - k:CONV_WIDTH - k, cs]
                v = _silu(acc)
                if c0 < d_inner:
                    y_scr[rows, cs] = v * dskip_ref[:, cs]
                    xr_scr[rows, cs] = v * dt_wide[rows, h0:h0 + CONV_COLS]
                elif c0 < d_inner + d_bc:
                    b0 = c0 - d_inner
                    bmt_scr[c * n_groups + b0 // SSD_STATE] = v.T.astype(BF16)
                else:
                    b0 = c0 - d_inner - d_bc
                    cm_scr[rows, b0:b0 + CONV_COLS] = v.astype(BF16)
                prev_tail = cur[q - CONV_CARRY:]
            convc_scr[:, cs] = prev_tail
            if packed:
                convo_ref[0, :, cs] = res[q - CONV_CARRY:]
            else:
                for j in range(CONV_WIDTH - 1):
                    r = _tile_row(lv - (CONV_WIDTH - 1) + j)
                    convo_ref[0, j:j + 1, cs] = res[r:r + 1]

    if packed:
        row_seq = lax.broadcasted_iota(jnp.int32, (q, SSD_STATE), 0) % SUBLANES
        col_seq = lax.broadcasted_iota(jnp.int32, (SSD_STATE, q), 1) % SUBLANES
        row_masks = [jnp.where(row_seq == s, 1.0, 0.0).astype(BF16) for s in range(nseq)]
        col_masks = [jnp.where(col_seq == s, 1.0, 0.0).astype(BF16) for s in range(nseq)]

    def chunk_body(c, carry):
        r0 = pl.multiple_of(c * q, q)
        crow = pl.ds(r0, q)
        for g in range(n_groups):
            gs = slice(g * GROUP_COLS, (g + 1) * GROUP_COLS)
            ns = slice(g * SSD_STATE, (g + 1) * SSD_STATE)
            cb_scr[g] = _dot(cm_scr[crow, ns], bmt_scr[c * n_groups + g])
            xr_bf = xr_scr[crow, gs].astype(BF16)
            for j in range(HEADS_PER_GROUP):
                ls = slice(j * SSD_HEAD_DIM, (j + 1) * SSD_HEAD_DIM)
                rcat_scr[g, j * q:(j + 1) * q, ls] = xr_bf[:, ls]
        acs = acs_scr[crow, :]
        acst = acst_scr[c]
        for g in range(n_groups):
            for j in range(HEADS_PER_GROUP):
                h = g * HEADS_PER_GROUP + j
                seg = acs[:, h:h + 1] - acst[h:h + 1, :]
                dec = jnp.exp2(jnp.where(causal, seg, -jnp.inf))
                mcat_scr[g, :, j * q:(j + 1) * q] = (cb_scr[g] * dec).astype(BF16)
        for g in range(n_groups):
            gs = slice(g * GROUP_COLS, (g + 1) * GROUP_COLS)
            ns = slice(g * SSD_STATE, (g + 1) * SSD_STATE)
            y_diag = _dot(mcat_scr[g], rcat_scr[g])
            xrd = (xr_scr[crow, gs] * dte_scr[crow, gs]).astype(BF16)
            if packed:
                cg = cm_scr[crow, ns]
                c_cat = jnp.concatenate([cg * row_masks[s] for s in range(nseq)], axis=1)
                st_cat = jnp.concatenate([st_scr[s, :, gs].astype(BF16) for s in range(nseq)], axis=0)
                y_off = _dot(c_cat, st_cat) * ea_scr[crow, gs]
                bt = bmt_scr[c * n_groups + g]
                for s in range(nseq):
                    cd = ea_scr[q - SUBLANES + s:q - SUBLANES + s + 1, gs]
                    st_scr[s, :, gs] = st_scr[s, :, gs] * cd + _dot(bt * col_masks[s], xrd)
            else:
                st = st_scr[0, :, gs]
                y_off = _dot(cm_scr[crow, ns], st.astype(BF16)) * ea_scr[crow, gs]
                cd = ea_scr[pl.ds(r0 + q - 1, 1), gs]
                st_scr[0, :, gs] = st * cd + _dot(bmt_scr[c * n_groups + g], xrd)
            y_scr[crow, gs] = y_scr[crow, gs] + y_diag + y_off
        return carry

    lax.fori_loop(0, n_chunks, chunk_body, 0)

    @pl.when(t == last_t)
    def _ssm_state():
        for s in range(nseq):
            for g in range(n_groups):
                cs = slice(g * GROUP_COLS, (g + 1) * GROUP_COLS)
                hout_ref[s, cs, :] = st_scr[s, :, cs].T

    for p0 in range(0, d_inner, PROJ_COLS):
        z_wide = _dot(u_ref[0], wz_ref[:, p0:p0 + PROJ_COLS])
        for h0 in range(0, PROJ_COLS, GROUP_COLS):
            gs = slice(p0 + h0, p0 + h0 + GROUP_COLS)
            yn_ref[0, :, gs] = _rms(y_scr[:, gs] * _silu(z_wide[:, h0:h0 + GROUP_COLS]),
                                    gssd_ref[:, gs]).astype(BF16)


def _ssd(x, mod, conv_hist, h0, p, *, lt, lv, nseq):
    bsz, lpad, d = x.shape
    nt = lpad // lt
    d_inner = p["wz"].shape[1]
    conv_dim = p["wxbc"].shape[1]
    n_groups = (conv_dim - d_inner) // (2 * SSD_STATE)
    n_chunks = lt // SSD_CHUNK
    hp = h0.shape[1]
    mod_rows = mod.shape[2]
    conv_rows = (CONV_WIDTH - 1) * nseq
    assert lv >= CONV_WIDTH - 1 and lt % SSD_CHUNK == 0 and d_inner == n_groups * GROUP_COLS
    assert nseq == 1 or (nseq == SUBLANES and lt == lv == lpad == SSD_CHUNK)

    weights = [p["gpre"], p["wz"], p["wxbc"], p["wdt"], p["convw"], p["convb"], p["dtb"], p["alog"],
               p["dskip"], p["gssd"], p["e2"]]
    in_specs = [
        pl.BlockSpec((1, lt, d), lambda b, t: (b, t, 0)),
        pl.BlockSpec((1, 6, mod_rows, d), lambda b, t: (b, 0, 0, 0)),
        pl.BlockSpec((1, CONV_CARRY, conv_dim), lambda b, t: (b, 0, 0)),
        pl.BlockSpec((nseq, hp, SSD_STATE), lambda b, t: (b, 0, 0)),
    ] + [pl.BlockSpec(memory_space=pltpu.VMEM) for _ in weights]
    out_shape = (
        jax.ShapeDtypeStruct((bsz, lpad, d), BF16),
        jax.ShapeDtypeStruct((bsz, lpad, d_inner), BF16),
        jax.ShapeDtypeStruct((bsz * nseq, hp, SSD_STATE), F32),
        jax.ShapeDtypeStruct((bsz, conv_rows, conv_dim), F32),
    )
    out_specs = (
        pl.BlockSpec((1, lt, d), lambda b, t: (b, t, 0)),
        pl.BlockSpec((1, lt, d_inner), lambda b, t: (b, t, 0)),
        pl.BlockSpec((nseq, hp, SSD_STATE), lambda b, t: (b, 0, 0)),
        pl.BlockSpec((1, conv_rows, conv_dim), lambda b, t: (b, 0, 0)),
    )
    scratch = [
        pltpu.VMEM((d // LANES, lt, LANES), F32),
        pltpu.VMEM((CONV_CARRY, conv_dim), F32),
        pltpu.VMEM((lt, d_inner), F32),
        pltpu.VMEM((n_chunks * n_groups, SSD_STATE, SSD_CHUNK), BF16),
        pltpu.VMEM((lt, n_groups * SSD_STATE), BF16),
        pltpu.VMEM((lt, LANES), F32),
        pltpu.VMEM((n_chunks, LANES, SSD_CHUNK), F32),
        pltpu.VMEM((lt, d_inner), F32),
        pltpu.VMEM((lt, d_inner), F32),
        pltpu.VMEM((lt, d_inner), F32),
        pltpu.VMEM((nseq, SSD_STATE, hp), F32),
        pltpu.VMEM((n_groups, SSD_CHUNK, SSD_CHUNK), F32),
        pltpu.VMEM((n_groups, HEADS_PER_GROUP * SSD_CHUNK, GROUP_COLS), BF16),
        pltpu.VMEM((n_groups, SSD_CHUNK, HEADS_PER_GROUP * SSD_CHUNK), BF16),
    ]
    kern = functools.partial(_ssd_kernel, lt=lt, lv=lv, d_inner=d_inner, n_groups=n_groups, nseq=nseq)
    return pl.pallas_call(
        kern, out_shape=out_shape, grid=(bsz, nt), in_specs=in_specs, out_specs=out_specs,
        scratch_shapes=scratch,
        compiler_params=pltpu.CompilerParams(dimension_semantics=("arbitrary", "arbitrary"),
                                             vmem_limit_bytes=VMEM_LIMIT),
        name="ssd",
    )(x, mod, conv_hist, h0, *weights)


def _merge_kernel(x_ref, u_ref, yn_ref, mod_ref, poolh_ref,
                  gpost_ref, wp_ref, wg_ref, wssd_ref, wpool_ref, pscale_ref, wo_ref,
                  out_ref, poolo_ref,
                  perm_scr, poolc_scr, ypool_scr,
                  *, lt, lv, pos0, nseq):
    q = SSD_CHUNK
    n_chunks = lt // q
    t = pl.program_id(1)
    packed = nseq > 1
    d_model = x_ref.shape[-1]
    n_slabs = d_model // LANES

    @pl.when(t == 0)
    def _init():
        poolc_scr[...] = poolh_ref[0]

    p_all = _dot(u_ref[0], wp_ref[...])
    for gi, w in enumerate(POOL_WINDOWS):
        gs = slice(gi * GROUP_COLS, (gi + 1) * GROUP_COLS)
        res = p_all[:, gs]
        prev_tail = poolc_scr[:, gs]
        for c in range(n_chunks):
            rows = slice(c * q, (c + 1) * q)
            cur = res[rows]
            back = _shifted_groups(prev_tail[POOL_CARRY - (w - 1) * SUBLANES:], cur, w - 1, packed)
            win = jnp.concatenate(back + [cur], axis=0)
            span = 1
            while span < w:
                win = win[span * SUBLANES:] + win[:win.shape[0] - span * SUBLANES]
                span *= 2
            if pos0 >= POOL_HIST or c > 0:
                pm = win * (1.0 / w) - cur
            else:
                cnt = jnp.minimum(_tokens((q, GROUP_COLS), 0) + (pos0 + 1), w).astype(F32)
                inv = jnp.where(t == 0, 1.0 / cnt, 1.0 / w)
                pm = win * inv - cur
            ypool_scr[rows, gs] = _dot(pm.astype(BF16), wpool_ref[gi]) * pscale_ref[:, gs]
            prev_tail = cur[SUBLANES:]
        poolc_scr[:, gs] = prev_tail
        if packed:
            poolo_ref[0, :, gs] = res[q - POOL_CARRY:]
        else:
            for j in range(POOL_HIST):
                r = _tile_row(lv - POOL_HIST + j)
                poolo_ref[0, j:j + 1, gs] = res[r:r + 1]

    y_ssd = _dot(yn_ref[0], wssd_ref[...])
    gate_ssd = _sigmoid(_dot(u_ref[0], wg_ref[:, 0:d_model]))
    gate_pool = _sigmoid(_dot(u_ref[0], wg_ref[:, d_model:2 * d_model]))
    merged = (gate_ssd * y_ssd + gate_pool * ypool_scr[...]).astype(BF16)
    mix = _dot(merged, wo_ref[...])
    branch = mod_ref[0, 2] * _rms(mix, gpost_ref[...])
    for c in range(n_chunks):
        for v in range(ROW_GROUPS):
            r0 = c * q + v * SUBLANES
            for j in range(n_slabs):
                perm_scr[j, pl.ds(c * q + v, SUBLANES, stride=ROW_GROUPS), :] = (
                    branch[r0:r0 + SUBLANES, j * LANES:(j + 1) * LANES])
    for j in range(n_slabs):
        ls = slice(j * LANES, (j + 1) * LANES)
        out_ref[0, :, ls] = x_ref[0, :, ls] + perm_scr[j]


def _merge(x, u, yn, mod, pool_hist, p, *, lt, lv, pos0, nseq):
    bsz, lpad, d = x.shape
    nt = lpad // lt
    d_inner = yn.shape[-1]
    d_pool = p["wp"].shape[1]
    mod_rows = mod.shape[2]
    pool_rows = POOL_HIST * nseq
    assert lv >= POOL_HIST and d_pool == len(POOL_WINDOWS) * GROUP_COLS
    assert nseq == 1 or (nseq == SUBLANES and lt == lv == lpad == SSD_CHUNK and pos0 >= POOL_HIST)
    weights = [p["gpost_mix"], p["wp"], p["wg"], p["wssd"], p["wpool"], p["pscale"], p["wo"]]
    in_specs = [
        pl.BlockSpec((1, lt, d), lambda b, t: (b, t, 0)),
        pl.BlockSpec((1, lt, d), lambda b, t: (b, t, 0)),
        pl.BlockSpec((1, lt, d_inner), lambda b, t: (b, t, 0)),
        pl.BlockSpec((1, 6, mod_rows, d), lambda b, t: (b, 0, 0, 0)),
        pl.BlockSpec((1, POOL_CARRY, d_pool), lambda b, t: (b, 0, 0)),
    ] + [pl.BlockSpec(memory_space=pltpu.VMEM) for _ in weights]
    out_shape = (
        jax.ShapeDtypeStruct((bsz, lpad, d), F32),
        jax.ShapeDtypeStruct((bsz, pool_rows, d_pool), F32),
    )
    out_specs = (
        pl.BlockSpec((1, lt, d), lambda b, t: (b, t, 0)),
        pl.BlockSpec((1, pool_rows, d_pool), lambda b, t: (b, 0, 0)),
    )
    scratch = [
        pltpu.VMEM((d // LANES, lt, LANES), F32),
        pltpu.VMEM((POOL_CARRY, d_pool), F32),
        pltpu.VMEM((lt, d_pool), F32),
    ]
    kern = functools.partial(_merge_kernel, lt=lt, lv=lv, pos0=pos0, nseq=nseq)
    return pl.pallas_call(
        kern, out_shape=out_shape, grid=(bsz, nt), in_specs=in_specs, out_specs=out_specs,
        scratch_shapes=scratch,
        compiler_params=pltpu.CompilerParams(dimension_semantics=("arbitrary", "arbitrary"),
                                             vmem_limit_bytes=VMEM_LIMIT),
        name="merge",
    )(x, u, yn, mod, pool_hist, *weights)


def _mlp_kernel(x_ref, mod_ref, gpre_ref, gpost_ref, wup_ref, wdown_ref, out_ref, v_scr):
    x = x_ref[0]
    v_scr[...] = (_rms(x, gpre_ref[...]) * (1.0 + mod_ref[0, 4]) + mod_ref[0, 3]).astype(BF16)
    acc = None
    for c0 in range(0, wup_ref.shape[1], PROJ_COLS):
        h = jnp.square(jnp.maximum(_dot(v_scr[...], wup_ref[:, c0:c0 + PROJ_COLS]), 0.0)).astype(BF16)
        part = _dot(h, wdown_ref[c0:c0 + PROJ_COLS, :])
        acc = part if acc is None else acc + part
    out_ref[0] = x + mod_ref[0, 5] * _rms(acc, gpost_ref[...])


def _mlp(x, mod, p, *, tm):
    bsz, lpad, d = x.shape
    nt = lpad // tm
    weights = [p["gpre_mlp"], p["gpost_mlp"], p["wup"], p["wdown"]]
    return pl.pallas_call(
        _mlp_kernel,
        out_shape=jax.ShapeDtypeStruct((bsz, lpad, d), F32),
        grid=(bsz, nt),
        in_specs=[pl.BlockSpec((1, tm, d), lambda b, t: (b, t, 0)),
                  pl.BlockSpec((1, 6, mod.shape[2], d), lambda b, t: (b, 0, 0, 0))]
                 + [pl.BlockSpec(memory_space=pltpu.VMEM) for _ in weights],
        out_specs=pl.BlockSpec((1, tm, d), lambda b, t: (b, t, 0)),
        scratch_shapes=[pltpu.VMEM((tm, d), BF16)],
        compiler_params=pltpu.CompilerParams(dimension_semantics=("arbitrary", "arbitrary"),
                                             vmem_limit_bytes=VMEM_LIMIT),
        name="mlp",
    )(x, mod, *weights)


def _prep_weights(g_pre_mix, g_post_mix, g_pre_mlp, g_post_mlp, w_in, conv_w, conv_b, dt_bias, a_log,
                  d_skip, g_ssd_norm, w_ssd_out, w_pool_group, pool_scale, w_o, w_up, w_down):
    d_inner = w_ssd_out.shape[0]
    conv_dim = conv_w.shape[1]
    n_heads = dt_bias.shape[0]
    d_pool = pool_scale.shape[0]
    s_z, s_xbc = d_inner, d_inner + conv_dim
    s_dt, s_pool = s_xbc + n_heads, s_xbc + n_heads + d_pool
    pad = LANES - n_heads
    head_of_col = jnp.arange(d_inner) // SSD_HEAD_DIM
    e1 = (jnp.arange(LANES)[:, None] == head_of_col[None, :]).astype(BF16)
    row = lambda a: a.reshape(1, -1).astype(F32)
    return dict(
        gpre=row(g_pre_mix), gpost_mix=row(g_post_mix), gpre_mlp=row(g_pre_mlp), gpost_mlp=row(g_post_mlp),
        wz=w_in[:, :s_z].astype(BF16), wxbc=w_in[:, s_z:s_xbc].astype(BF16),
        wdt=jnp.pad(w_in[:, s_xbc:s_dt], ((0, 0), (0, pad))).astype(BF16),
        wp=w_in[:, s_dt:s_pool].astype(BF16), wg=w_in[:, s_pool:].astype(BF16),
        convw=conv_w.astype(F32), convb=row(conv_b),
        dtb=jnp.pad(row(dt_bias), ((0, 0), (0, pad))), alog=jnp.pad(row(a_log), ((0, 0), (0, pad))),
        dskip=jnp.repeat(row(d_skip), SSD_HEAD_DIM, axis=1), gssd=row(g_ssd_norm),
        e2=jnp.concatenate([e1, e1], axis=0),
        wssd=w_ssd_out.astype(BF16), wpool=w_pool_group.astype(BF16), pscale=row(pool_scale),
        wo=w_o.astype(BF16), wup=w_up.astype(BF16), wdown=w_down.astype(BF16),
    )


def _history_rows(hist):
    bsz, n, c = hist.shape
    out = jnp.zeros((bsz, n, SUBLANES, c), hist.dtype).at[:, :, SUBLANES - 1, :].set(hist)
    return out.reshape(bsz, n * SUBLANES, c)


def _layer(x, mod, conv_hist, pool_hist, h0, p, *, pos0):
    bsz, seq, d = x.shape
    heads, hdim, nstate = h0.shape[1:]
    h0 = h0.reshape(bsz, heads * hdim, nstate)
    if bsz == SUBLANES and seq == ROW_GROUPS and pos0 >= POOL_HIST:
        return _layer_packed(x, mod, conv_hist, pool_hist, h0, p, pos0=pos0, state_shape=(heads, hdim, nstate))
    if seq % MIX_TILE == 0:
        lt, lpad = MIX_TILE, seq
    else:
        lt = lpad = -(-seq // SSD_CHUNK) * SSD_CHUNK
    lv = seq - (lpad - lt)
    xp = x if lpad == seq else jnp.pad(x, ((0, 0), (0, lpad - seq), (0, 0)))
    mod = mod[:, :, None, :]
    u, yn, h_new, conv_state = _ssd(xp, mod, _history_rows(conv_hist), h0, p, lt=lt, lv=lv, nseq=1)
    x1, pool_state = _merge(xp, u, yn, mod, _history_rows(pool_hist), p, lt=lt, lv=lv, pos0=pos0, nseq=1)
    tm = MLP_TILE if lpad % MLP_TILE == 0 else lpad
    y = _mlp(x1, mod, p, tm=tm)
    if lpad != seq:
        y = y[:, :seq]
    return y, h_new.reshape(bsz, heads, hdim, nstate), conv_state, pool_state


def _layer_packed(x, mod, conv_hist, pool_hist, h0, p, *, pos0, state_shape):
    bsz, seq, d = x.shape
    rows = bsz * seq
    by_step = lambda a: a.transpose(1, 0, 2).reshape(1, a.shape[1] * bsz, a.shape[2])
    from_step = lambda a, n: a.reshape(n, bsz, a.shape[-1]).transpose(1, 0, 2)
    mod_t = mod.transpose(1, 0, 2)
    mod_tok = jnp.repeat(mod_t, seq, axis=1)[None]
    mod_perm = jnp.tile(mod_t, (1, seq, 1))[None]
    xp = x.reshape(1, rows, d)
    u, yn, h_new, conv_state = _ssd(xp, mod_tok, by_step(conv_hist), h0, p, lt=rows, lv=rows, nseq=bsz)
    x1, pool_state = _merge(xp, u, yn, mod_perm, by_step(pool_hist), p, lt=rows, lv=rows, pos0=pos0, nseq=bsz)
    y = _mlp(x1, mod_tok, p, tm=rows)
    return (y.reshape(bsz, seq, d), h_new.reshape((bsz,) + state_shape),
            from_step(conv_state, CONV_WIDTH - 1), from_step(pool_state, POOL_HIST))


def kernel(x_prompt, x_sample, state_ssm, state_conv, state_pool, c_prompt, c_sample, w_ada, b_ada, g_pre_mix, g_post_mix, g_pre_mlp, g_post_mlp, w_in, conv_w, conv_b, dt_bias, a_log, d_skip, g_ssd_norm, w_ssd_out, w_pool_group, pool_scale, w_o, w_up, w_down):
    yp, ys = x_prompt, x_sample
    bp, d = x_prompt.shape[0], x_prompt.shape[2]
    bs = x_sample.shape[0]
    depth = w_in.shape[0]
    outs = [[] for _ in range(6)]
    for l in range(depth):
        p = _prep_weights(g_pre_mix[l], g_post_mix[l], g_pre_mlp[l], g_post_mlp[l], w_in[l], conv_w[l],
                          conv_b[l], dt_bias[l], a_log[l], d_skip[l], g_ssd_norm[l], w_ssd_out[l],
                          w_pool_group[l], pool_scale[l], w_o[l], w_up[l], w_down[l])
        mod = _adaln(jnp.concatenate([c_prompt, c_sample], axis=0), w_ada[l], b_ada[l])
        mod_p = mod[:bp].reshape(bp, 6, d)
        mod_s = mod[bp:].reshape(bs, 6, d)
        zc = jnp.zeros((bp,) + state_conv.shape[2:], yp.dtype)
        zp = jnp.zeros((bp,) + state_pool.shape[2:], yp.dtype)
        zh = jnp.zeros((bp,) + state_ssm.shape[2:], yp.dtype)
        yp, h_p, c_p, p_p = _layer(yp, mod_p, zc, zp, zh, p, pos0=0)
        ys, h_s, c_s, p_s = _layer(ys, mod_s, state_conv[l], state_pool[l], state_ssm[l], p, pos0=PAST_LEN)
        for lst, val in zip(outs, (h_p, c_p, p_p, h_s, c_s, p_s)):
            lst.append(val)
    return (yp, ys) + tuple(jnp.stack(o) for o in outs)
```

```python
import functools

import jax
import jax.numpy as jnp
from jax import lax
from jax.experimental import pallas as pl
from jax.experimental.pallas import tpu as pltpu

F32 = jnp.float32
BF16 = jnp.bfloat16

EPS = 1e-6
LOG2E = 1.4426950408889634
PAST_LEN = 2048
POOL_WINDOWS = (2, 4, 8, 16)
POOL_HIST = max(POOL_WINDOWS) - 1
CONV_WIDTH = 4
SSD_HEAD_DIM = 64
SSD_STATE = 128
LANES = 128
SUBLANES = 8
SSD_CHUNK = 128
ROW_GROUPS = SSD_CHUNK // SUBLANES
GROUP_COLS = 256
HEADS_PER_GROUP = GROUP_COLS // SSD_HEAD_DIM
PROJ_COLS = 512
CONV_COLS = SSD_STATE
CONV_CARRY = (CONV_WIDTH - 1) * SUBLANES
POOL_CARRY = POOL_HIST * SUBLANES
MIX_TILE = 512
MLP_TILE = 512
VMEM_LIMIT = 60 * 1024 * 1024


def _row_of_token(tok):
    return (tok % ROW_GROUPS) * SUBLANES + tok // ROW_GROUPS


def _tile_row(tok):
    return (tok // SSD_CHUNK) * SSD_CHUNK + _row_of_token(tok % SSD_CHUNK)


def _tokens(shape, axis):
    r = lax.broadcasted_iota(jnp.int32, shape, axis)
    return (r // SSD_CHUNK) * SSD_CHUNK + (r % SUBLANES) * ROW_GROUPS + (r % SSD_CHUNK) // SUBLANES


def _dot(a, b):
    return jnp.dot(a, b, preferred_element_type=F32)


def _dot_wt(a, w_t):
    return lax.dot_general(a, w_t, (((1,), (1,)), ((), ())), preferred_element_type=F32)


def _sigmoid(v):
    return 1.0 / (1.0 + jnp.exp2(v * (-LOG2E)))


def _silu(v):
    return v * _sigmoid(v)


def _silu_of_twice(h):
    return h + h * jnp.tanh(h)


def _rms(v, g):
    return v * lax.rsqrt(jnp.mean(v * v, axis=-1, keepdims=True) + EPS) * g


def _shifted_groups(prev_tail, cur, n_back, packed):
    if packed:
        return [prev_tail[j * SUBLANES:(j + 1) * SUBLANES] for j in range(n_back)]
    first = lax.broadcasted_iota(jnp.int32, (SUBLANES, cur.shape[1]), 0) == 0
    tail = cur[SSD_CHUNK - n_back * SUBLANES:]
    parts = []
    for j in range(n_back):
        rs = slice(j * SUBLANES, (j + 1) * SUBLANES)
        parts.append(jnp.where(first, pltpu.roll(prev_tail[rs], 1, axis=0), pltpu.roll(tail[rs], 1, axis=0)))
    return parts


def _adaln_kernel(c_ref, w_ref, b_ref, o_ref):
    c = c_ref[...]
    o_ref[...] = jnp.dot(_silu(c), w_ref[...], preferred_element_type=F32,
                         precision=lax.Precision.HIGHEST) + b_ref[...]


def _adaln(c, w_ada, b_ada):
    rows, d = c.shape
    n = w_ada.shape[1]
    bn = d
    return pl.pallas_call(
        _adaln_kernel,
        out_shape=jax.ShapeDtypeStruct((rows, n), F32),
        grid=(n // bn,),
        in_specs=[pl.BlockSpec((rows, d), lambda j: (0, 0)),
                  pl.BlockSpec((d, bn), lambda j: (0, j)),
                  pl.BlockSpec((1, bn), lambda j: (0, j))],
        out_specs=pl.BlockSpec((rows, bn), lambda j: (0, j)),
        compiler_params=pltpu.CompilerParams(dimension_semantics=("arbitrary",)),
        name="adaln",
    )(c, w_ada, b_ada.reshape(1, n))


def _ssd_kernel(x_ref, mod_ref, convh_ref, h0_ref,
                gpre_ref, wz_ref, wxbc_ref, wdt_ref, convw_ref, convb_ref, dtb_ref, alog_ref, dskip_ref,
                gssd_ref, e2_ref,
                u_ref, yn_ref, hout_ref, convo_ref,
                perm_scr, convc_scr, xr_scr, bmt_scr, cm_scr, acs_scr, acst_scr,
                ea_scr, dte_scr, y_scr, st_scr, cb_scr, rcat_scr, mcat_scr,
                *, lt, lv, d_inner, n_groups, nseq):
    q = SSD_CHUNK
    n_chunks = lt // q
    t = pl.program_id(1)
    last_t = pl.num_programs(1) - 1
    d_bc = n_groups * SSD_STATE
    d_model = x_ref.shape[-1]
    n_slabs = d_model // LANES
    packed = nseq > 1

    @pl.when(t == 0)
    def _init():
        convc_scr[...] = convh_ref[0]
        rcat_scr[...] = jnp.zeros(rcat_scr.shape, BF16)
        for s in range(nseq):
            for g in range(n_groups):
                cs = slice(g * GROUP_COLS, (g + 1) * GROUP_COLS)
                st_scr[s, :, cs] = h0_ref[s, cs, :].T

    u = _rms(x_ref[0], gpre_ref[...]) * (1.0 + mod_ref[0, 1]) + mod_ref[0, 0]
    for c in range(n_chunks):
        for i in range(ROW_GROUPS):
            start = c * q + (i // 2 if i % 2 == 0 else q // 2 + i // 2)
            r0 = c * q + i * SUBLANES
            for j in range(n_slabs):
                perm_scr[j, pl.ds(start, SUBLANES, stride=SUBLANES), :] = (
                    u[r0:r0 + SUBLANES, j * LANES:(j + 1) * LANES])
    for j in range(n_slabs):
        u_ref[0, :, j * LANES:(j + 1) * LANES] = perm_scr[j].astype(BF16)

    lane = lax.broadcasted_iota(jnp.int32, (1, LANES), 1)
    n_heads = d_inner // SSD_HEAD_DIM
    dt = jax.nn.softplus(_dot_wt(u_ref[0], wdt_ref[...]) + dtb_ref[...])
    if lv < lt:
        dt = jnp.where(_tokens((lt, LANES), 0) < lv, dt, 0.0)
    a_row = jnp.where(lane < n_heads, -jnp.exp(alog_ref[...]), 0.0)
    da = dt * a_row
    tok_l, tok_s = _tokens((q, q), 0), _tokens((q, q), 1)
    causal = tok_l >= tok_s
    if packed:
        causal = causal & (tok_l // ROW_GROUPS == tok_s // ROW_GROUPS)
    tri = causal.astype(F32)
    ea_parts, dte_parts = [], []
    for c in range(n_chunks):
        acs = jnp.dot(tri, da[c * q:(c + 1) * q], preferred_element_type=F32,
                      precision=lax.Precision.HIGHEST)
        acs_scr[c * q:(c + 1) * q, :] = acs * LOG2E
        acst_scr[c] = (acs * LOG2E).T
        ea_parts.append(jnp.exp(acs))
        last = jnp.concatenate([acs[q - SUBLANES:]] * ROW_GROUPS, axis=0) if packed else acs[q - 1:q, :]
        dte_parts.append(jnp.exp(last - acs))

    def split_hi_lo(v):
        hi = v.astype(BF16)
        lo = (v - hi.astype(F32)).astype(BF16)
        return jnp.concatenate([hi, lo], axis=1)

    dt2 = split_hi_lo(dt)
    ea2 = split_hi_lo(jnp.concatenate(ea_parts, axis=0))
    dte2 = split_hi_lo(jnp.concatenate(dte_parts, axis=0))
    for c0 in range(0, d_inner, PROJ_COLS):
        cs = slice(c0, c0 + PROJ_COLS)
        ea_scr[:, cs] = _dot(ea2, e2_ref[:, cs])
        dte_scr[:, cs] = _dot(dte2, e2_ref[:, cs])

    for p0 in range(0, d_inner + 2 * d_bc, PROJ_COLS):
        ps = slice(p0, p0 + PROJ_COLS)
        res_wide = _dot_wt(u_ref[0], wxbc_ref[ps, :])
        dt_wide = _dot(dt2, e2_ref[:, ps]) if p0 < d_inner else None
        for h0 in range(0, PROJ_COLS, CONV_COLS):
            c0 = p0 + h0
            cs = slice(c0, c0 + CONV_COLS)
            res = res_wide[:, h0:h0 + CONV_COLS]
            prev_tail = convc_scr[:, cs]
            for c in range(n_chunks):
                rows = slice(c * q, (c + 1) * q)
                cur = res[rows]
                ext = jnp.concatenate(_shifted_groups(prev_tail, cur, CONV_WIDTH - 1, packed) + [cur], axis=0)
                acc = convb_ref[:, cs] + ext[CONV_CARRY:] * convw_ref[CONV_WIDTH - 1:CONV_WIDTH, cs]
                for k in range(1, CONV_WIDTH):
                    lo = CONV_CARRY - k * SUBLANES
                    acc = acc + ext[lo:lo + q] * convw_ref[CONV_WIDTH - 1 - k:CONV_WIDTH - k, cs]
                v = _silu_of_twice(acc)
                if c0 < d_inner:
                    y_scr[rows, cs] = v * dskip_ref[:, cs]
                    xr_scr[rows, cs] = v * dt_wide[rows, h0:h0 + CONV_COLS]
                elif c0 < d_inner + d_bc:
                    b0 = c0 - d_inner
                    bmt_scr[c * n_groups + b0 // SSD_STATE] = v.T.astype(BF16)
                else:
                    b0 = c0 - d_inner - d_bc
                    cm_scr[rows, b0:b0 + CONV_COLS] = v.astype(BF16)
                prev_tail = cur[q - CONV_CARRY:]
            convc_scr[:, cs] = prev_tail
            if packed:
                convo_ref[0, :, cs] = res[q - CONV_CARRY:]
            else:
                for j in range(CONV_WIDTH - 1):
                    r = _tile_row(lv - (CONV_WIDTH - 1) + j)
                    convo_ref[0, j:j + 1, cs] = res[r:r + 1]

    if packed:
        row_seq = lax.broadcasted_iota(jnp.int32, (q, SSD_STATE), 0) % SUBLANES
        col_seq = lax.broadcasted_iota(jnp.int32, (SSD_STATE, q), 1) % SUBLANES
        row_masks = [jnp.where(row_seq == s, 1.0, 0.0).astype(BF16) for s in range(nseq)]
        col_masks = [jnp.where(col_seq == s, 1.0, 0.0).astype(BF16) for s in range(nseq)]

    def chunk_body(c, carry):
        r0 = pl.multiple_of(c * q, q)
        crow = pl.ds(r0, q)
        for g in range(n_groups):
            gs = slice(g * GROUP_COLS, (g + 1) * GROUP_COLS)
            ns = slice(g * SSD_STATE, (g + 1) * SSD_STATE)
            cb_scr[g] = _dot(cm_scr[crow, ns], bmt_scr[c * n_groups + g])
            xr_bf = xr_scr[crow, gs].astype(BF16)
            for j in range(HEADS_PER_GROUP):
                ls = slice(j * SSD_HEAD_DIM, (j + 1) * SSD_HEAD_DIM)
                rcat_scr[g, j * q:(j + 1) * q, ls] = xr_bf[:, ls]
        acs = acs_scr[crow, :]
        acst = acst_scr[c]
        for g in range(n_groups):
            for j in range(HEADS_PER_GROUP):
                h = g * HEADS_PER_GROUP + j
                seg = acs[:, h:h + 1] - acst[h:h + 1, :]
                dec = jnp.exp2(jnp.where(causal, seg, -jnp.inf))
                mcat_scr[g, :, j * q:(j + 1) * q] = (cb_scr[g] * dec).astype(BF16)
        for g in range(n_groups):
            gs = slice(g * GROUP_COLS, (g + 1) * GROUP_COLS)
            ns = slice(g * SSD_STATE, (g + 1) * SSD_STATE)
            y_diag = _dot(mcat_scr[g], rcat_scr[g])
            xrd = (xr_scr[crow, gs] * dte_scr[crow, gs]).astype(BF16)
            if packed:
                cg = cm_scr[crow, ns]
                c_cat = jnp.concatenate([cg * row_masks[s] for s in range(nseq)], axis=1)
                st_cat = jnp.concatenate([st_scr[s, :, gs].astype(BF16) for s in range(nseq)], axis=0)
                y_off = _dot(c_cat, st_cat) * ea_scr[crow, gs]
                bt = bmt_scr[c * n_groups + g]
                for s in range(nseq):
                    cd = ea_scr[q - SUBLANES + s:q - SUBLANES + s + 1, gs]
                    st_scr[s, :, gs] = st_scr[s, :, gs] * cd + _dot(bt * col_masks[s], xrd)
            else:
                st = st_scr[0, :, gs]
                y_off = _dot(cm_scr[crow, ns], st.astype(BF16)) * ea_scr[crow, gs]
                cd = ea_scr[pl.ds(r0 + q - 1, 1), gs]
                st_scr[0, :, gs] = st * cd + _dot(bmt_scr[c * n_groups + g], xrd)
            y_scr[crow, gs] = y_scr[crow, gs] + y_diag + y_off
        return carry

    lax.fori_loop(0, n_chunks, chunk_body, 0)

    @pl.when(t == last_t)
    def _ssm_state():
        for s in range(nseq):
            for g in range(n_groups):
                cs = slice(g * GROUP_COLS, (g + 1) * GROUP_COLS)
                hout_ref[s, cs, :] = st_scr[s, :, cs].T

    for p0 in range(0, d_inner, PROJ_COLS):
        z_wide = _dot_wt(u_ref[0], wz_ref[p0:p0 + PROJ_COLS, :])
        for h0 in range(0, PROJ_COLS, GROUP_COLS):
            gs = slice(p0 + h0, p0 + h0 + GROUP_COLS)
            yn_ref[0, :, gs] = _rms(y_scr[:, gs] * _silu_of_twice(z_wide[:, h0:h0 + GROUP_COLS]),
                                    gssd_ref[:, gs]).astype(BF16)


def _ssd(x, mod, conv_hist, h0, p, *, lt, lv, nseq):
    bsz, lpad, d = x.shape
    nt = lpad // lt
    d_inner = p["wz"].shape[0]
    conv_dim = p["wxbc"].shape[0]
    n_groups = (conv_dim - d_inner) // (2 * SSD_STATE)
    n_chunks = lt // SSD_CHUNK
    hp = h0.shape[1]
    mod_rows = mod.shape[2]
    conv_rows = (CONV_WIDTH - 1) * nseq
    assert lv >= CONV_WIDTH - 1 and lt % SSD_CHUNK == 0 and d_inner == n_groups * GROUP_COLS
    assert nseq == 1 or (nseq == SUBLANES and lt == lv == lpad == SSD_CHUNK)

    weights = [p["gpre"], p["wz"], p["wxbc"], p["wdt"], p["convw"], p["convb"], p["dtb"], p["alog"],
               p["dskip"], p["gssd"], p["e2"]]
    in_specs = [
        pl.BlockSpec((1, lt, d), lambda b, t: (b, t, 0)),
        pl.BlockSpec((1, 6, mod_rows, d), lambda b, t: (b, 0, 0, 0)),
        pl.BlockSpec((1, CONV_CARRY, conv_dim), lambda b, t: (b, 0, 0)),
        pl.BlockSpec((nseq, hp, SSD_STATE), lambda b, t: (b, 0, 0)),
    ] + [pl.BlockSpec(memory_space=pltpu.VMEM) for _ in weights]
    out_shape = (
        jax.ShapeDtypeStruct((bsz, lpad, d), BF16),
        jax.ShapeDtypeStruct((bsz, lpad, d_inner), BF16),
        jax.ShapeDtypeStruct((bsz * nseq, hp, SSD_STATE), F32),
        jax.ShapeDtypeStruct((bsz, conv_rows, conv_dim), F32),
    )
    out_specs = (
        pl.BlockSpec((1, lt, d), lambda b, t: (b, t, 0)),
        pl.BlockSpec((1, lt, d_inner), lambda b, t: (b, t, 0)),
        pl.BlockSpec((nseq, hp, SSD_STATE), lambda b, t: (b, 0, 0)),
        pl.BlockSpec((1, conv_rows, conv_dim), lambda b, t: (b, 0, 0)),
    )
    scratch = [
        pltpu.VMEM((d // LANES, lt, LANES), F32),
        pltpu.VMEM((CONV_CARRY, conv_dim), F32),
        pltpu.VMEM((lt, d_inner), F32),
        pltpu.VMEM((n_chunks * n_groups, SSD_STATE, SSD_CHUNK), BF16),
        pltpu.VMEM((lt, n_groups * SSD_STATE), BF16),
        pltpu.VMEM((lt, LANES), F32),
        pltpu.VMEM((n_chunks, LANES, SSD_CHUNK), F32),
        pltpu.VMEM((lt, d_inner), F32),
        pltpu.VMEM((lt, d_inner), F32),
        pltpu.VMEM((lt, d_inner), F32),
        pltpu.VMEM((nseq, SSD_STATE, hp), F32),
        pltpu.VMEM((n_groups, SSD_CHUNK, SSD_CHUNK), F32),
        pltpu.VMEM((n_groups, HEADS_PER_GROUP * SSD_CHUNK, GROUP_COLS), BF16),
        pltpu.VMEM((n_groups, SSD_CHUNK, HEADS_PER_GROUP * SSD_CHUNK), BF16),
    ]
    kern = functools.partial(_ssd_kernel, lt=lt, lv=lv, d_inner=d_inner, n_groups=n_groups, nseq=nseq)
    return pl.pallas_call(
        kern, out_shape=out_shape, grid=(bsz, nt), in_specs=in_specs, out_specs=out_specs,
        scratch_shapes=scratch,
        compiler_params=pltpu.CompilerParams(dimension_semantics=("arbitrary", "arbitrary"),
                                             vmem_limit_bytes=VMEM_LIMIT),
        name="ssd",
    )(x, mod, conv_hist, h0, *weights)


def _merge_kernel(x_ref, u_ref, yn_ref, mod_ref, poolh_ref,
                  gpost_ref, wp_ref, wg_ref, wssd_ref, wpool_ref, pscale_ref, wo_ref,
                  out_ref, poolo_ref,
                  perm_scr, poolc_scr, ypool_scr,
                  *, lt, lv, pos0, nseq):
    q = SSD_CHUNK
    n_chunks = lt // q
    t = pl.program_id(1)
    packed = nseq > 1
    d_model = x_ref.shape[-1]
    n_slabs = d_model // LANES

    @pl.when(t == 0)
    def _init():
        poolc_scr[...] = poolh_ref[0]

    p_all = _dot_wt(u_ref[0], wp_ref[...])
    for gi, w in enumerate(POOL_WINDOWS):
        gs = slice(gi * GROUP_COLS, (gi + 1) * GROUP_COLS)
        res = p_all[:, gs]
        prev_tail = poolc_scr[:, gs]
        for c in range(n_chunks):
            rows = slice(c * q, (c + 1) * q)
            cur = res[rows]
            back = _shifted_groups(prev_tail[POOL_CARRY - (w - 1) * SUBLANES:], cur, w - 1, packed)
            win = jnp.concatenate(back + [cur], axis=0)
            span = 1
            while span < w:
                win = win[span * SUBLANES:] + win[:win.shape[0] - span * SUBLANES]
                span *= 2
            if pos0 >= POOL_HIST or c > 0:
                pm = win * (1.0 / w) - cur
            else:
                cnt = jnp.minimum(_tokens((q, GROUP_COLS), 0) + (pos0 + 1), w).astype(F32)
                inv = jnp.where(t == 0, 1.0 / cnt, 1.0 / w)
                pm = win * inv - cur
            ypool_scr[rows, gs] = _dot(pm.astype(BF16), wpool_ref[gi]) * pscale_ref[:, gs]
            prev_tail = cur[SUBLANES:]
        poolc_scr[:, gs] = prev_tail
        if packed:
            poolo_ref[0, :, gs] = res[q - POOL_CARRY:]
        else:
            for j in range(POOL_HIST):
                r = _tile_row(lv - POOL_HIST + j)
                poolo_ref[0, j:j + 1, gs] = res[r:r + 1]

    y_ssd = _dot(yn_ref[0], wssd_ref[...])
    gate_ssd = _sigmoid(_dot_wt(u_ref[0], wg_ref[0:d_model, :]))
    gate_pool = _sigmoid(_dot_wt(u_ref[0], wg_ref[d_model:2 * d_model, :]))
    merged = (gate_ssd * y_ssd + gate_pool * ypool_scr[...]).astype(BF16)
    mix = _dot(merged, wo_ref[...])
    branch = mod_ref[0, 2] * _rms(mix, gpost_ref[...])
    for j in range(n_slabs):
        perm_scr[j] = branch[:, j * LANES:(j + 1) * LANES]
    for c in range(n_chunks):
        for i in range(ROW_GROUPS):
            start = c * q + (i // 2 if i % 2 == 0 else q // 2 + i // 2)
            rows = slice(c * q + i * SUBLANES, c * q + (i + 1) * SUBLANES)
            for j in range(n_slabs):
                ls = slice(j * LANES, (j + 1) * LANES)
                out_ref[0, rows, ls] = x_ref[0, rows, ls] + perm_scr[j, pl.ds(start, SUBLANES, stride=SUBLANES), :]


def _merge(x, u, yn, mod, pool_hist, p, *, lt, lv, pos0, nseq):
    bsz, lpad, d = x.shape
    nt = lpad // lt
    d_inner = yn.shape[-1]
    d_pool = p["wp"].shape[0]
    mod_rows = mod.shape[2]
    pool_rows = POOL_HIST * nseq
    assert lv >= POOL_HIST and d_pool == len(POOL_WINDOWS) * GROUP_COLS
    assert nseq == 1 or (nseq == SUBLANES and lt == lv == lpad == SSD_CHUNK and pos0 >= POOL_HIST)
    weights = [p["gpost_mix"], p["wp"], p["wg"], p["wssd"], p["wpool"], p["pscale"], p["wo"]]
    in_specs = [
        pl.BlockSpec((1, lt, d), lambda b, t: (b, t, 0)),
        pl.BlockSpec((1, lt, d), lambda b, t: (b, t, 0)),
        pl.BlockSpec((1, lt, d_inner), lambda b, t: (b, t, 0)),
        pl.BlockSpec((1, 6, mod_rows, d), lambda b, t: (b, 0, 0, 0)),
        pl.BlockSpec((1, POOL_CARRY, d_pool), lambda b, t: (b, 0, 0)),
    ] + [pl.BlockSpec(memory_space=pltpu.VMEM) for _ in weights]
    out_shape = (
        jax.ShapeDtypeStruct((bsz, lpad, d), F32),
        jax.ShapeDtypeStruct((bsz, pool_rows, d_pool), F32),
    )
    out_specs = (
        pl.BlockSpec((1, lt, d), lambda b, t: (b, t, 0)),
        pl.BlockSpec((1, pool_rows, d_pool), lambda b, t: (b, 0, 0)),
    )
    scratch = [
        pltpu.VMEM((d // LANES, lt, LANES), F32),
        pltpu.VMEM((POOL_CARRY, d_pool), F32),
        pltpu.VMEM((lt, d_pool), F32),
    ]
    kern = functools.partial(_merge_kernel, lt=lt, lv=lv, pos0=pos0, nseq=nseq)
    return pl.pallas_call(
        kern, out_shape=out_shape, grid=(bsz, nt), in_specs=in_specs, out_specs=out_specs,
        scratch_shapes=scratch,
        compiler_params=pltpu.CompilerParams(dimension_semantics=("arbitrary", "arbitrary"),
                                             vmem_limit_bytes=VMEM_LIMIT),
        name="merge",
    )(x, u, yn, mod, pool_hist, *weights)


def _mlp_kernel(x_ref, mod_ref, gpre_ref, gpost_ref, wup_ref, wdown_ref, out_ref, v_scr):
    x = x_ref[0]
    v_scr[...] = (_rms(x, gpre_ref[...]) * (1.0 + mod_ref[0, 4]) + mod_ref[0, 3]).astype(BF16)
    acc = None
    for c0 in range(0, wup_ref.shape[1], PROJ_COLS):
        h = jnp.square(jnp.maximum(_dot(v_scr[...], wup_ref[:, c0:c0 + PROJ_COLS]), 0.0)).astype(BF16)
        part = _dot(h, wdown_ref[c0:c0 + PROJ_COLS, :])
        acc = part if acc is None else acc + part
    out_ref[0] = x + mod_ref[0, 5] * _rms(acc, gpost_ref[...])


def _mlp(x, mod, p, *, tm):
    bsz, lpad, d = x.shape
    nt = lpad // tm
    weights = [p["gpre_mlp"], p["gpost_mlp"], p["wup"], p["wdown"]]
    return pl.pallas_call(
        _mlp_kernel,
        out_shape=jax.ShapeDtypeStruct((bsz, lpad, d), F32),
        grid=(bsz, nt),
        in_specs=[pl.BlockSpec((1, tm, d), lambda b, t: (b, t, 0)),
                  pl.BlockSpec((1, 6, mod.shape[2], d), lambda b, t: (b, 0, 0, 0))]
                 + [pl.BlockSpec(memory_space=pltpu.VMEM) for _ in weights],
        out_specs=pl.BlockSpec((1, tm, d), lambda b, t: (b, t, 0)),
        scratch_shapes=[pltpu.VMEM((tm, d), BF16)],
        compiler_params=pltpu.CompilerParams(dimension_semantics=("arbitrary", "arbitrary"),
                                             vmem_limit_bytes=VMEM_LIMIT),
        name="mlp",
    )(x, mod, *weights)


def _prep_weights(g_pre_mix, g_post_mix, g_pre_mlp, g_post_mlp, w_in, conv_w, conv_b, dt_bias, a_log,
                  d_skip, g_ssd_norm, w_ssd_out, w_pool_group, pool_scale, w_o, w_up, w_down):
    d_inner = w_ssd_out.shape[0]
    conv_dim = conv_w.shape[1]
    n_heads = dt_bias.shape[0]
    d_pool = pool_scale.shape[0]
    s_z, s_xbc = d_inner, d_inner + conv_dim
    s_dt, s_pool = s_xbc + n_heads, s_xbc + n_heads + d_pool
    pad = LANES - n_heads
    head_of_col = jnp.arange(d_inner) // SSD_HEAD_DIM
    e1 = (jnp.arange(LANES)[:, None] == head_of_col[None, :]).astype(BF16)
    row = lambda a: a.reshape(1, -1).astype(F32)
    w_t = jnp.swapaxes(w_in, 0, 1)
    return dict(
        gpre=row(g_pre_mix), gpost_mix=row(g_post_mix), gpre_mlp=row(g_pre_mlp), gpost_mlp=row(g_post_mlp),
        wz=(0.5 * w_t[:s_z]).astype(BF16), wxbc=w_t[s_z:s_xbc].astype(BF16),
        wdt=jnp.pad(w_t[s_xbc:s_dt], ((0, pad), (0, 0))).astype(BF16),
        wp=w_t[s_dt:s_pool].astype(BF16), wg=w_t[s_pool:].astype(BF16),
        convw=0.5 * conv_w.astype(F32), convb=0.5 * row(conv_b),
        dtb=jnp.pad(row(dt_bias), ((0, 0), (0, pad))), alog=jnp.pad(row(a_log), ((0, 0), (0, pad))),
        dskip=jnp.repeat(row(d_skip), SSD_HEAD_DIM, axis=1), gssd=row(g_ssd_norm),
        e2=jnp.concatenate([e1, e1], axis=0),
        wssd=w_ssd_out.astype(BF16), wpool=w_pool_group.astype(BF16), pscale=row(pool_scale),
        wo=w_o.astype(BF16), wup=w_up.astype(BF16), wdown=w_down.astype(BF16),
    )


def _history_rows(hist):
    bsz, n, c = hist.shape
    out = jnp.zeros((bsz, n, SUBLANES, c), hist.dtype).at[:, :, SUBLANES - 1, :].set(hist)
    return out.reshape(bsz, n * SUBLANES, c)


def _layer(x, mod, conv_hist, pool_hist, h0, p, *, pos0):
    bsz, seq, d = x.shape
    heads, hdim, nstate = h0.shape[1:]
    h0 = h0.reshape(bsz, heads * hdim, nstate)
    if bsz == SUBLANES and seq == ROW_GROUPS and pos0 >= POOL_HIST:
        return _layer_packed(x, mod, conv_hist, pool_hist, h0, p, pos0=pos0, state_shape=(heads, hdim, nstate))
    if seq % MIX_TILE == 0:
        lt, lpad = MIX_TILE, seq
    else:
        lt = lpad = -(-seq // SSD_CHUNK) * SSD_CHUNK
    lv = seq - (lpad - lt)
    xp = x if lpad == seq else jnp.pad(x, ((0, 0), (0, lpad - seq), (0, 0)))
    mod = mod[:, :, None, :]
    u, yn, h_new, conv_state = _ssd(xp, mod, _history_rows(conv_hist), h0, p, lt=lt, lv=lv, nseq=1)
    x1, pool_state = _merge(xp, u, yn, mod, _history_rows(pool_hist), p, lt=lt, lv=lv, pos0=pos0, nseq=1)
    tm = MLP_TILE if lpad % MLP_TILE == 0 else lpad
    y = _mlp(x1, mod, p, tm=tm)
    if lpad != seq:
        y = y[:, :seq]
    return y, h_new.reshape(bsz, heads, hdim, nstate), conv_state, pool_state


def _layer_packed(x, mod, conv_hist, pool_hist, h0, p, *, pos0, state_shape):
    bsz, seq, d = x.shape
    rows = bsz * seq
    by_step = lambda a: a.transpose(1, 0, 2).reshape(1, a.shape[1] * bsz, a.shape[2])
    from_step = lambda a, n: a.reshape(n, bsz, a.shape[-1]).transpose(1, 0, 2)
    mod_t = mod.transpose(1, 0, 2)
    mod_tok = jnp.repeat(mod_t, seq, axis=1)[None]
    mod_perm = jnp.tile(mod_t, (1, seq, 1))[None]
    xp = x.reshape(1, rows, d)
    u, yn, h_new, conv_state = _ssd(xp, mod_tok, by_step(conv_hist), h0, p, lt=rows, lv=rows, nseq=bsz)
    x1, pool_state = _merge(xp, u, yn, mod_perm, by_step(pool_hist), p, lt=rows, lv=rows, pos0=pos0, nseq=bsz)
    y = _mlp(x1, mod_tok, p, tm=rows)
    return (y.reshape(bsz, seq, d), h_new.reshape((bsz,) + state_shape),
            from_step(conv_state, CONV_WIDTH - 1), from_step(pool_state, POOL_HIST))


def kernel(x_prompt, x_sample, state_ssm, state_conv, state_pool, c_prompt, c_sample, w_ada, b_ada, g_pre_mix, g_post_mix, g_pre_mlp, g_post_mlp, w_in, conv_w, conv_b, dt_bias, a_log, d_skip, g_ssd_norm, w_ssd_out, w_pool_group, pool_scale, w_o, w_up, w_down):
    yp, ys = x_prompt, x_sample
    bp, d = x_prompt.shape[0], x_prompt.shape[2]
    bs = x_sample.shape[0]
    depth = w_in.shape[0]
    outs = [[] for _ in range(6)]
    for l in range(depth):
        p = _prep_weights(g_pre_mix[l], g_post_mix[l], g_pre_mlp[l], g_post_mlp[l], w_in[l], conv_w[l],
                          conv_b[l], dt_bias[l], a_log[l], d_skip[l], g_ssd_norm[l], w_ssd_out[l],
                          w_pool_group[l], pool_scale[l], w_o[l], w_up[l], w_down[l])
        mod = _adaln(jnp.concatenate([c_prompt, c_sample], axis=0), w_ada[l], b_ada[l])
        mod_p = mod[:bp].reshape(bp, 6, d)
        mod_s = mod[bp:].reshape(bs, 6, d)
        zc = jnp.zeros((bp,) + state_conv.shape[2:], yp.dtype)
        zp = jnp.zeros((bp,) + state_pool.shape[2:], yp.dtype)
        zh = jnp.zeros((bp,) + state_ssm.shape[2:], yp.dtype)
        yp, h_p, c_p, p_p = _layer(yp, mod_p, zc, zp, zh, p, pos0=0)
        ys, h_s, c_s, p_s = _layer(ys, mod_s, state_conv[l], state_pool[l], state_ssm[l], p, pos0=PAST_LEN)
        for lst, val in zip(outs, (h_p, c_p, p_p, h_s, c_s, p_s)):
            lst.append(val)
    return (yp, ys) + tuple(jnp.stack(o) for o in outs)
```

```python
import functools

import jax
import jax.numpy as jnp
from jax import lax
from jax.experimental import pallas as pl
from jax.experimental.pallas import tpu as pltpu

F32 = jnp.float32
BF16 = jnp.bfloat16

EPS = 1e-6
LOG2E = 1.4426950408889634
PAST_LEN = 2048
POOL_WINDOWS = (2, 4, 8, 16)
POOL_HIST = max(POOL_WINDOWS) - 1
CONV_WIDTH = 4
SSD_HEAD_DIM = 64
SSD_STATE = 128
LANES = 128
SUBLANES = 8
SSD_CHUNK = 128
ROW_GROUPS = SSD_CHUNK // SUBLANES
GROUP_COLS = 256
HEADS_PER_GROUP = GROUP_COLS // SSD_HEAD_DIM
PROJ_COLS = 512
CONV_COLS = SSD_STATE
CONV_CARRY = (CONV_WIDTH - 1) * SUBLANES
POOL_CARRY = POOL_HIST * SUBLANES
MIX_TILE = 512
MERGE_TILE = 1024
MLP_TILE = 1024
VMEM_LIMIT = 60 * 1024 * 1024


def _row_of_token(tok):
    return (tok % ROW_GROUPS) * SUBLANES + tok // ROW_GROUPS


def _tile_row(tok):
    return (tok // SSD_CHUNK) * SSD_CHUNK + _row_of_token(tok % SSD_CHUNK)


def _tokens(shape, axis):
    r = lax.broadcasted_iota(jnp.int32, shape, axis)
    return (r // SSD_CHUNK) * SSD_CHUNK + (r % SUBLANES) * ROW_GROUPS + (r % SSD_CHUNK) // SUBLANES


def _dot(a, b):
    return jnp.dot(a, b, preferred_element_type=F32)


def _dot_wt(a, w_t):
    return lax.dot_general(a, w_t, (((1,), (1,)), ((), ())), preferred_element_type=F32)


def _sigmoid(v):
    return 1.0 / (1.0 + jnp.exp2(v * (-LOG2E)))


def _silu(v):
    return v * _sigmoid(v)


def _silu_of_twice(h):
    return h + h * jnp.tanh(h)


def _rms(v, g):
    return v * lax.rsqrt(jnp.mean(v * v, axis=-1, keepdims=True) + EPS) * g


def _shifted_groups(prev_tail, cur, n_back, packed):
    if packed:
        return [prev_tail[j * SUBLANES:(j + 1) * SUBLANES] for j in range(n_back)]
    first = lax.broadcasted_iota(jnp.int32, (SUBLANES, cur.shape[1]), 0) == 0
    tail = cur[SSD_CHUNK - n_back * SUBLANES:]
    parts = []
    for j in range(n_back):
        rs = slice(j * SUBLANES, (j + 1) * SUBLANES)
        parts.append(jnp.where(first, pltpu.roll(prev_tail[rs], 1, axis=0), pltpu.roll(tail[rs], 1, axis=0)))
    return parts


def _adaln_kernel(c_ref, w_ref, b_ref, o_ref):
    c = c_ref[...]
    o_ref[...] = jnp.dot(_silu(c), w_ref[...], preferred_element_type=F32,
                         precision=lax.Precision.HIGHEST) + b_ref[...]


def _adaln(c, w_ada, b_ada):
    rows, d = c.shape
    n = w_ada.shape[1]
    bn = d
    return pl.pallas_call(
        _adaln_kernel,
        out_shape=jax.ShapeDtypeStruct((rows, n), F32),
        grid=(n // bn,),
        in_specs=[pl.BlockSpec((rows, d), lambda j: (0, 0)),
                  pl.BlockSpec((d, bn), lambda j: (0, j)),
                  pl.BlockSpec((1, bn), lambda j: (0, j))],
        out_specs=pl.BlockSpec((rows, bn), lambda j: (0, j)),
        compiler_params=pltpu.CompilerParams(dimension_semantics=("arbitrary",)),
        name="adaln",
    )(c, w_ada, b_ada.reshape(1, n))


def _ssd_kernel(x_ref, mod_ref, convh_ref, h0_ref,
                gpre_ref, win_ref, convw_ref, convb_ref, dtb_ref, alog_ref, dskip_ref,
                gssd_ref, e2_ref,
                u_ref, yn_ref, hout_ref, convo_ref,
                perm_scr, convc_scr, xr_scr, bmt_scr, cm_scr, acs_scr, acst_scr,
                ea_scr, dte_scr, y_scr, st_scr, cb_scr, rcat_scr, mcat_scr,
                *, lt, lv, d_inner, n_groups, nseq):
    q = SSD_CHUNK
    n_chunks = lt // q
    d_bc = n_groups * SSD_STATE
    d_model = x_ref.shape[-1]
    n_slabs = d_model // LANES
    packed = nseq > 1

    def prenorm(dst_ref):
        u = _rms(x_ref[0], gpre_ref[...]) * (1.0 + mod_ref[0, 1]) + mod_ref[0, 0]
        for c in range(n_chunks):
            for i in range(ROW_GROUPS):
                start = c * q + (i // 2 if i % 2 == 0 else q // 2 + i // 2)
                r0 = c * q + i * SUBLANES
                for j in range(n_slabs):
                    perm_scr[j, pl.ds(start, SUBLANES, stride=SUBLANES), :] = (
                        u[r0:r0 + SUBLANES, j * LANES:(j + 1) * LANES])
        for j in range(n_slabs):
            dst_ref[:, j * LANES:(j + 1) * LANES] = perm_scr[j].astype(BF16)

    def ssd_tile(t):
        last_t = pl.num_programs(1) - 1

        @pl.when(t == 0)
        def _init():
            convc_scr[...] = convh_ref[0]
            rcat_scr[...] = jnp.zeros(rcat_scr.shape, BF16)
            for s in range(nseq):
                for g in range(n_groups):
                    cs = slice(g * GROUP_COLS, (g + 1) * GROUP_COLS)
                    st_scr[s, :, cs] = h0_ref[s, cs, :].T

        prenorm(u_ref.at[0])

        lane = lax.broadcasted_iota(jnp.int32, (1, LANES), 1)
        n_heads = d_inner // SSD_HEAD_DIM
        r_dt = d_inner + d_inner + 2 * d_bc
        dt = jax.nn.softplus(_dot_wt(u_ref[0], win_ref[r_dt:r_dt + LANES, :]) + dtb_ref[...])
        if lv < lt:
            dt = jnp.where(_tokens((lt, LANES), 0) < lv, dt, 0.0)
        a_row = jnp.where(lane < n_heads, -jnp.exp(alog_ref[...]), 0.0)
        da = dt * a_row
        tok_l, tok_s = _tokens((q, q), 0), _tokens((q, q), 1)
        causal = tok_l >= tok_s
        if packed:
            causal = causal & (tok_l // ROW_GROUPS == tok_s // ROW_GROUPS)
        tri = causal.astype(F32)
        ea_parts, dte_parts = [], []
        for c in range(n_chunks):
            acs = jnp.dot(tri, da[c * q:(c + 1) * q], preferred_element_type=F32,
                          precision=lax.Precision.HIGHEST)
            acs_scr[c * q:(c + 1) * q, :] = acs * LOG2E
            acst_scr[c] = (acs * LOG2E).T
            ea_parts.append(jnp.exp(acs))
            last = jnp.concatenate([acs[q - SUBLANES:]] * ROW_GROUPS, axis=0) if packed else acs[q - 1:q, :]
            dte_parts.append(jnp.exp(last - acs))

        def split_hi_lo(v):
            hi = v.astype(BF16)
            lo = (v - hi.astype(F32)).astype(BF16)
            return jnp.concatenate([hi, lo], axis=1)

        dt2 = split_hi_lo(dt)
        ea2 = split_hi_lo(jnp.concatenate(ea_parts, axis=0))
        dte2 = split_hi_lo(jnp.concatenate(dte_parts, axis=0))
        for c0 in range(0, d_inner, PROJ_COLS):
            cs = slice(c0, c0 + PROJ_COLS)
            ea_scr[:, cs] = _dot(ea2, e2_ref[:, cs])
            dte_scr[:, cs] = _dot(dte2, e2_ref[:, cs])

        for p0 in range(0, d_inner + 2 * d_bc, PROJ_COLS):
            ps = slice(p0, p0 + PROJ_COLS)
            res_wide = _dot_wt(u_ref[0], win_ref[d_inner + p0:d_inner + p0 + PROJ_COLS, :])
            dt_wide = _dot(dt2, e2_ref[:, ps]) if p0 < d_inner else None
            for h0 in range(0, PROJ_COLS, CONV_COLS):
                c0 = p0 + h0
                cs = slice(c0, c0 + CONV_COLS)
                res = res_wide[:, h0:h0 + CONV_COLS]
                prev_tail = convc_scr[:, cs]
                for c in range(n_chunks):
                    rows = slice(c * q, (c + 1) * q)
                    cur = res[rows]
                    ext = jnp.concatenate(_shifted_groups(prev_tail, cur, CONV_WIDTH - 1, packed) + [cur], axis=0)
                    acc = convb_ref[:, cs] + ext[CONV_CARRY:] * convw_ref[CONV_WIDTH - 1:CONV_WIDTH, cs]
                    for k in range(1, CONV_WIDTH):
                        lo = CONV_CARRY - k * SUBLANES
                        acc = acc + ext[lo:lo + q] * convw_ref[CONV_WIDTH - 1 - k:CONV_WIDTH - k, cs]
                    v = _silu_of_twice(acc)
                    if c0 < d_inner:
                        y_scr[rows, cs] = v * dskip_ref[:, cs]
                        xr_scr[rows, cs] = v * dt_wide[rows, h0:h0 + CONV_COLS]
                    elif c0 < d_inner + d_bc:
                        b0 = c0 - d_inner
                        bmt_scr[c * n_groups + b0 // SSD_STATE] = v.T.astype(BF16)
                    else:
                        b0 = c0 - d_inner - d_bc
                        cm_scr[rows, b0:b0 + CONV_COLS] = v.astype(BF16)
                    prev_tail = cur[q - CONV_CARRY:]
                convc_scr[:, cs] = prev_tail
                if packed:
                    convo_ref[0, :, cs] = res[q - CONV_CARRY:]
                else:
                    for j in range(CONV_WIDTH - 1):
                        r = _tile_row(lv - (CONV_WIDTH - 1) + j)
                        convo_ref[0, j:j + 1, cs] = res[r:r + 1]

        if packed:
            row_seq = lax.broadcasted_iota(jnp.int32, (q, SSD_STATE), 0) % SUBLANES
            col_seq = lax.broadcasted_iota(jnp.int32, (SSD_STATE, q), 1) % SUBLANES
            row_masks = [jnp.where(row_seq == s, 1.0, 0.0).astype(BF16) for s in range(nseq)]
            col_masks = [jnp.where(col_seq == s, 1.0, 0.0).astype(BF16) for s in range(nseq)]

        def chunk_body(c, carry):
            r0 = pl.multiple_of(c * q, q)
            crow = pl.ds(r0, q)
            for g in range(n_groups):
                gs = slice(g * GROUP_COLS, (g + 1) * GROUP_COLS)
                ns = slice(g * SSD_STATE, (g + 1) * SSD_STATE)
                cb_scr[g] = _dot(cm_scr[crow, ns], bmt_scr[c * n_groups + g])
                xr_bf = xr_scr[crow, gs].astype(BF16)
                for j in range(HEADS_PER_GROUP):
                    ls = slice(j * SSD_HEAD_DIM, (j + 1) * SSD_HEAD_DIM)
                    rcat_scr[g, j * q:(j + 1) * q, ls] = xr_bf[:, ls]
            acs = acs_scr[crow, :]
            acst = acst_scr[c]
            for g in range(n_groups):
                for j in range(HEADS_PER_GROUP):
                    h = g * HEADS_PER_GROUP + j
                    seg = acs[:, h:h + 1] - acst[h:h + 1, :]
                    dec = jnp.exp2(jnp.where(causal, seg, -jnp.inf))
                    mcat_scr[g, :, j * q:(j + 1) * q] = (cb_scr[g] * dec).astype(BF16)
            for g in range(n_groups):
                gs = slice(g * GROUP_COLS, (g + 1) * GROUP_COLS)
                ns = slice(g * SSD_STATE, (g + 1) * SSD_STATE)
                y_diag = _dot(mcat_scr[g], rcat_scr[g])
                xrd = (xr_scr[crow, gs] * dte_scr[crow, gs]).astype(BF16)
                if packed:
                    cg = cm_scr[crow, ns]
                    c_cat = jnp.concatenate([cg * row_masks[s] for s in range(nseq)], axis=1)
                    st_cat = jnp.concatenate([st_scr[s, :, gs].astype(BF16) for s in range(nseq)], axis=0)
                    y_off = _dot(c_cat, st_cat) * ea_scr[crow, gs]
                    bt = bmt_scr[c * n_groups + g]
                    for s in range(nseq):
                        cd = ea_scr[q - SUBLANES + s:q - SUBLANES + s + 1, gs]
                        st_scr[s, :, gs] = st_scr[s, :, gs] * cd + _dot(bt * col_masks[s], xrd)
                else:
                    st = st_scr[0, :, gs]
                    y_off = _dot(cm_scr[crow, ns], st.astype(BF16)) * ea_scr[crow, gs]
                    cd = ea_scr[pl.ds(r0 + q - 1, 1), gs]
                    st_scr[0, :, gs] = st * cd + _dot(bmt_scr[c * n_groups + g], xrd)
                y_scr[crow, gs] = y_scr[crow, gs] + y_diag + y_off
            return carry

        lax.fori_loop(0, n_chunks, chunk_body, 0)

        @pl.when(t == last_t)
        def _ssm_state():
            for s in range(nseq):
                for g in range(n_groups):
                    cs = slice(g * GROUP_COLS, (g + 1) * GROUP_COLS)
                    hout_ref[s, cs, :] = st_scr[s, :, cs].T

        for p0 in range(0, d_inner, PROJ_COLS):
            z_wide = _dot_wt(u_ref[0], win_ref[p0:p0 + PROJ_COLS, :])
            for h0 in range(0, PROJ_COLS, GROUP_COLS):
                gs = slice(p0 + h0, p0 + h0 + GROUP_COLS)
                yn_ref[0, :, gs] = _rms(y_scr[:, gs] * _silu_of_twice(z_wide[:, h0:h0 + GROUP_COLS]),
                                        gssd_ref[:, gs]).astype(BF16)

    ssd_tile(pl.program_id(1))


def _ssd(x, mod, conv_hist, h0, p, *, lt, lv, nseq):
    bsz, lpad, d = x.shape
    nt = lpad // lt
    d_inner = p["dskip"].shape[1]
    conv_dim = p["convw"].shape[1]
    n_groups = (conv_dim - d_inner) // (2 * SSD_STATE)
    n_chunks = lt // SSD_CHUNK
    hp = h0.shape[1]
    mod_rows = mod.shape[2]
    conv_rows = (CONV_WIDTH - 1) * nseq
    assert lv >= CONV_WIDTH - 1 and lt % SSD_CHUNK == 0 and d_inner == n_groups * GROUP_COLS
    assert nseq == 1 or (nseq == SUBLANES and lt == lv == lpad == SSD_CHUNK)

    weights = [p["gpre"], p["win_ssd"], p["convw"], p["convb"], p["dtb"], p["alog"],
               p["dskip"], p["gssd"], p["e2"]]
    tile_map = lambda b, t: (b, t, 0)
    seq_map = lambda b, t: (b, 0, 0)
    shared_map = lambda b, t: (0, 0, 0)
    in_specs = [
        pl.BlockSpec((1, lt, d), tile_map),
        pl.BlockSpec((1, 6, mod_rows, d), lambda b, t: (b, 0, 0, 0)),
        pl.BlockSpec((1, CONV_CARRY, conv_dim), seq_map if conv_hist.shape[0] == bsz else shared_map),
        pl.BlockSpec((nseq, hp, SSD_STATE), seq_map if h0.shape[0] == bsz * nseq else shared_map),
    ] + [pl.BlockSpec(memory_space=pltpu.VMEM) for _ in weights]
    out_shape = (
        jax.ShapeDtypeStruct((bsz, lpad, d), BF16),
        jax.ShapeDtypeStruct((bsz, lpad, d_inner), BF16),
        jax.ShapeDtypeStruct((bsz * nseq, hp, SSD_STATE), F32),
        jax.ShapeDtypeStruct((bsz, conv_rows, conv_dim), F32),
    )
    out_specs = (
        pl.BlockSpec((1, lt, d), tile_map),
        pl.BlockSpec((1, lt, d_inner), tile_map),
        pl.BlockSpec((nseq, hp, SSD_STATE), seq_map),
        pl.BlockSpec((1, conv_rows, conv_dim), seq_map),
    )
    scratch = [
        pltpu.VMEM((d // LANES, lt, LANES), F32),
        pltpu.VMEM((CONV_CARRY, conv_dim), F32),
        pltpu.VMEM((lt, d_inner), F32),
        pltpu.VMEM((n_chunks * n_groups, SSD_STATE, SSD_CHUNK), BF16),
        pltpu.VMEM((lt, n_groups * SSD_STATE), BF16),
        pltpu.VMEM((lt, LANES), F32),
        pltpu.VMEM((n_chunks, LANES, SSD_CHUNK), F32),
        pltpu.VMEM((lt, d_inner), F32),
        pltpu.VMEM((lt, d_inner), F32),
        pltpu.VMEM((lt, d_inner), F32),
        pltpu.VMEM((nseq, SSD_STATE, hp), F32),
        pltpu.VMEM((n_groups, SSD_CHUNK, SSD_CHUNK), F32),
        pltpu.VMEM((n_groups, HEADS_PER_GROUP * SSD_CHUNK, GROUP_COLS), BF16),
        pltpu.VMEM((n_groups, SSD_CHUNK, HEADS_PER_GROUP * SSD_CHUNK), BF16),
    ]
    kern = functools.partial(_ssd_kernel, lt=lt, lv=lv, d_inner=d_inner, n_groups=n_groups, nseq=nseq)
    return pl.pallas_call(
        kern, out_shape=out_shape, grid=(bsz, nt), in_specs=in_specs, out_specs=out_specs,
        scratch_shapes=scratch,
        compiler_params=pltpu.CompilerParams(dimension_semantics=("arbitrary", "arbitrary"),
                                             vmem_limit_bytes=VMEM_LIMIT),
        name="ssd",
    )(x, mod, conv_hist, h0, *weights)


def _merge_kernel(x_ref, u_ref, yn_ref, mod_ref, poolh_ref,
                  gpost_ref, win_ref, wssd_ref, wpool_ref, pscale_ref, wo_ref,
                  out_ref, poolo_ref,
                  perm_scr, poolc_scr, ypool_scr,
                  *, lt, lv, pos0, nseq):
    q = SSD_CHUNK
    n_chunks = lt // q
    t = pl.program_id(1)
    packed = nseq > 1
    d_model = x_ref.shape[-1]
    n_slabs = d_model // LANES

    @pl.when(t == 0)
    def _init():
        poolc_scr[...] = poolh_ref[0]

    d_pool = ypool_scr.shape[1]
    p_all = _dot_wt(u_ref[0], win_ref[0:d_pool, :])
    for gi, w in enumerate(POOL_WINDOWS):
        gs = slice(gi * GROUP_COLS, (gi + 1) * GROUP_COLS)
        res = p_all[:, gs]
        prev_tail = poolc_scr[:, gs]
        for c in range(n_chunks):
            rows = slice(c * q, (c + 1) * q)
            cur = res[rows]
            back = _shifted_groups(prev_tail[POOL_CARRY - (w - 1) * SUBLANES:], cur, w - 1, packed)
            win = jnp.concatenate(back + [cur], axis=0)
            span = 1
            while span < w:
                win = win[span * SUBLANES:] + win[:win.shape[0] - span * SUBLANES]
                span *= 2
            if pos0 >= POOL_HIST or c > 0:
                pm = win * (1.0 / w) - cur
            else:
                cnt = jnp.minimum(_tokens((q, GROUP_COLS), 0) + (pos0 + 1), w).astype(F32)
                inv = jnp.where(t == 0, 1.0 / cnt, 1.0 / w)
                pm = win * inv - cur
            ypool_scr[rows, gs] = _dot(pm.astype(BF16), wpool_ref[gi]) * pscale_ref[:, gs]
            prev_tail = cur[SUBLANES:]
        poolc_scr[:, gs] = prev_tail
        if packed:
            poolo_ref[0, :, gs] = res[q - POOL_CARRY:]
        else:
            for j in range(POOL_HIST):
                r = _tile_row(lv - POOL_HIST + j)
                poolo_ref[0, j:j + 1, gs] = res[r:r + 1]

    y_ssd = _dot(yn_ref[0], wssd_ref[...])
    gate_ssd = _sigmoid(_dot_wt(u_ref[0], win_ref[d_pool:d_pool + d_model, :]))
    gate_pool = _sigmoid(_dot_wt(u_ref[0], win_ref[d_pool + d_model:d_pool + 2 * d_model, :]))
    merged = (gate_ssd * y_ssd + gate_pool * ypool_scr[...]).astype(BF16)
    mix = _dot(merged, wo_ref[...])
    branch = mod_ref[0, 2] * _rms(mix, gpost_ref[...])
    for j in range(n_slabs):
        perm_scr[j] = branch[:, j * LANES:(j + 1) * LANES]
    for c in range(n_chunks):
        for i in range(ROW_GROUPS):
            start = c * q + (i // 2 if i % 2 == 0 else q // 2 + i // 2)
            rows = slice(c * q + i * SUBLANES, c * q + (i + 1) * SUBLANES)
            for j in range(n_slabs):
                ls = slice(j * LANES, (j + 1) * LANES)
                out_ref[0, rows, ls] = x_ref[0, rows, ls] + perm_scr[j, pl.ds(start, SUBLANES, stride=SUBLANES), :]


def _merge(x, u, yn, mod, pool_hist, p, *, lt, lv, pos0, nseq):
    bsz, lpad, d = x.shape
    nt = lpad // lt
    d_inner = yn.shape[-1]
    d_pool = p["pscale"].shape[1]
    mod_rows = mod.shape[2]
    pool_rows = POOL_HIST * nseq
    assert lv >= POOL_HIST and d_pool == len(POOL_WINDOWS) * GROUP_COLS
    assert nseq == 1 or (nseq == SUBLANES and lt == lv == lpad == SSD_CHUNK and pos0 >= POOL_HIST)
    weights = [p["gpost_mix"], p["win_merge"], p["wssd"], p["wpool"], p["pscale"], p["wo"]]
    in_specs = [
        pl.BlockSpec((1, lt, d), lambda b, t: (b, t, 0)),
        pl.BlockSpec((1, lt, d), lambda b, t: (b, t, 0)),
        pl.BlockSpec((1, lt, d_inner), lambda b, t: (b, t, 0)),
        pl.BlockSpec((1, 6, mod_rows, d), lambda b, t: (b, 0, 0, 0)),
        pl.BlockSpec((1, POOL_CARRY, d_pool), lambda b, t: (b if pool_hist.shape[0] == bsz else 0, 0, 0)),
    ] + [pl.BlockSpec(memory_space=pltpu.VMEM) for _ in weights]
    out_shape = (
        jax.ShapeDtypeStruct((bsz, lpad, d), F32),
        jax.ShapeDtypeStruct((bsz, pool_rows, d_pool), F32),
    )
    out_specs = (
        pl.BlockSpec((1, lt, d), lambda b, t: (b, t, 0)),
        pl.BlockSpec((1, pool_rows, d_pool), lambda b, t: (b, 0, 0)),
    )
    scratch = [
        pltpu.VMEM((d // LANES, lt, LANES), F32),
        pltpu.VMEM((POOL_CARRY, d_pool), F32),
        pltpu.VMEM((lt, d_pool), F32),
    ]
    kern = functools.partial(_merge_kernel, lt=lt, lv=lv, pos0=pos0, nseq=nseq)
    return pl.pallas_call(
        kern, out_shape=out_shape, grid=(bsz, nt), in_specs=in_specs, out_specs=out_specs,
        scratch_shapes=scratch,
        compiler_params=pltpu.CompilerParams(dimension_semantics=("arbitrary", "arbitrary"),
                                             vmem_limit_bytes=VMEM_LIMIT),
        name="merge",
    )(x, u, yn, mod, pool_hist, *weights)


def _mlp_kernel(x_ref, mod_ref, gpre_ref, gpost_ref, wup_ref, wdown_ref, out_ref, v_scr):
    x = x_ref[0]
    v_scr[...] = (_rms(x, gpre_ref[...]) * (1.0 + mod_ref[0, 4]) + mod_ref[0, 3]).astype(BF16)
    acc = None
    for c0 in range(0, wup_ref.shape[1], PROJ_COLS):
        h = jnp.square(jnp.maximum(_dot(v_scr[...], wup_ref[:, c0:c0 + PROJ_COLS]), 0.0)).astype(BF16)
        part = _dot(h, wdown_ref[c0:c0 + PROJ_COLS, :])
        acc = part if acc is None else acc + part
    out_ref[0] = x + mod_ref[0, 5] * _rms(acc, gpost_ref[...])


def _mlp(x, mod, p, *, tm):
    bsz, lpad, d = x.shape
    nt = lpad // tm
    weights = [p["gpre_mlp"], p["gpost_mlp"], p["wup"], p["wdown"]]
    return pl.pallas_call(
        _mlp_kernel,
        out_shape=jax.ShapeDtypeStruct((bsz, lpad, d), F32),
        grid=(bsz, nt),
        in_specs=[pl.BlockSpec((1, tm, d), lambda b, t: (b, t, 0)),
                  pl.BlockSpec((1, 6, mod.shape[2], d), lambda b, t: (b, 0, 0, 0))]
                 + [pl.BlockSpec(memory_space=pltpu.VMEM) for _ in weights],
        out_specs=pl.BlockSpec((1, tm, d), lambda b, t: (b, t, 0)),
        scratch_shapes=[pltpu.VMEM((tm, d), BF16)],
        compiler_params=pltpu.CompilerParams(dimension_semantics=("arbitrary", "arbitrary"),
                                             vmem_limit_bytes=VMEM_LIMIT),
        name="mlp",
    )(x, mod, *weights)


def _prep_weights(g_pre_mix, g_post_mix, g_pre_mlp, g_post_mlp, w_in, conv_w, conv_b, dt_bias, a_log,
                  d_skip, g_ssd_norm, w_ssd_out, w_pool_group, pool_scale, w_o, w_up, w_down):
    d_inner = w_ssd_out.shape[0]
    conv_dim = conv_w.shape[1]
    n_heads = dt_bias.shape[0]
    d_pool = pool_scale.shape[0]
    s_z, s_xbc = d_inner, d_inner + conv_dim
    s_dt, s_pool = s_xbc + n_heads, s_xbc + n_heads + d_pool
    pad = LANES - n_heads
    head_of_col = jnp.arange(d_inner) // SSD_HEAD_DIM
    e1 = (jnp.arange(LANES)[:, None] == head_of_col[None, :]).astype(BF16)
    row = lambda a: a.reshape(1, -1).astype(F32)
    z_rows = (jnp.arange(w_in.shape[1]) < s_z)[:, None]
    w_bf = (jnp.swapaxes(w_in, 0, 1) * jnp.where(z_rows, 0.5, 1.0)).astype(BF16)
    return dict(
        gpre=row(g_pre_mix), gpost_mix=row(g_post_mix), gpre_mlp=row(g_pre_mlp), gpost_mlp=row(g_post_mlp),
        win_ssd=w_bf[:s_xbc + LANES], win_merge=w_bf[s_dt:],
        convw=0.5 * conv_w.astype(F32), convb=0.5 * row(conv_b),
        dtb=jnp.pad(row(dt_bias), ((0, 0), (0, pad))), alog=jnp.pad(row(a_log), ((0, 0), (0, pad))),
        dskip=jnp.repeat(row(d_skip), SSD_HEAD_DIM, axis=1), gssd=row(g_ssd_norm),
        e2=jnp.concatenate([e1, e1], axis=0),
        wssd=w_ssd_out.astype(BF16), wpool=w_pool_group.astype(BF16), pscale=row(pool_scale),
        wo=w_o.astype(BF16), wup=w_up.astype(BF16), wdown=w_down.astype(BF16),
    )


def _history_rows(hist):
    bsz, n, c = hist.shape
    out = jnp.zeros((bsz, n, SUBLANES, c), hist.dtype).at[:, :, SUBLANES - 1, :].set(hist)
    return out.reshape(bsz, n * SUBLANES, c)


def _layer(x, mod, conv_hist, pool_hist, h0, p, *, pos0):
    bsz, seq, d = x.shape
    heads, hdim, nstate = h0.shape[1:]
    h0 = h0.reshape(h0.shape[0], heads * hdim, nstate)
    if bsz == SUBLANES and seq == ROW_GROUPS and pos0 >= POOL_HIST:
        return _layer_packed(x, mod, conv_hist, pool_hist, h0, p, pos0=pos0, state_shape=(heads, hdim, nstate))
    if seq % MIX_TILE == 0:
        lt, lpad = MIX_TILE, seq
    else:
        lt = lpad = -(-seq // SSD_CHUNK) * SSD_CHUNK
    lv = seq - (lpad - lt)
    xp = x if lpad == seq else jnp.pad(x, ((0, 0), (0, lpad - seq), (0, 0)))
    mod = mod[:, :, None, :]
    u, yn, h_new, conv_state = _ssd(xp, mod, _history_rows(conv_hist), h0, p, lt=lt, lv=lv, nseq=1)
    lm = MERGE_TILE if lpad % MERGE_TILE == 0 else lt
    x1, pool_state = _merge(xp, u, yn, mod, _history_rows(pool_hist), p, lt=lm, lv=lv + lm - lt, pos0=pos0, nseq=1)
    tm = MLP_TILE if lpad % MLP_TILE == 0 else lpad
    y = _mlp(x1, mod, p, tm=tm)
    if lpad != seq:
        y = y[:, :seq]
    return y, h_new.reshape(bsz, heads, hdim, nstate), conv_state, pool_state


def _layer_packed(x, mod, conv_hist, pool_hist, h0, p, *, pos0, state_shape):
    bsz, seq, d = x.shape
    rows = bsz * seq
    by_step = lambda a: a.transpose(1, 0, 2).reshape(1, a.shape[1] * bsz, a.shape[2])
    from_step = lambda a, n: a.reshape(n, bsz, a.shape[-1]).transpose(1, 0, 2)
    mod_t = mod.transpose(1, 0, 2)
    mod_tok = jnp.repeat(mod_t, seq, axis=1)[None]
    mod_perm = jnp.tile(mod_t, (1, seq, 1))[None]
    xp = x.reshape(1, rows, d)
    u, yn, h_new, conv_state = _ssd(xp, mod_tok, by_step(conv_hist), h0, p, lt=rows, lv=rows, nseq=bsz)
    x1, pool_state = _merge(xp, u, yn, mod_perm, by_step(pool_hist), p, lt=rows, lv=rows, pos0=pos0, nseq=bsz)
    y = _mlp(x1, mod_tok, p, tm=rows)
    return (y.reshape(bsz, seq, d), h_new.reshape((bsz,) + state_shape),
            from_step(conv_state, CONV_WIDTH - 1), from_step(pool_state, POOL_HIST))


def kernel(x_prompt, x_sample, state_ssm, state_conv, state_pool, c_prompt, c_sample, w_ada, b_ada, g_pre_mix, g_post_mix, g_pre_mlp, g_post_mlp, w_in, conv_w, conv_b, dt_bias, a_log, d_skip, g_ssd_norm, w_ssd_out, w_pool_group, pool_scale, w_o, w_up, w_down):
    yp, ys = x_prompt, x_sample
    bp, d = x_prompt.shape[0], x_prompt.shape[2]
    bs = x_sample.shape[0]
    depth = w_in.shape[0]
    outs = [[] for _ in range(6)]
    for l in range(depth):
        p = _prep_weights(g_pre_mix[l], g_post_mix[l], g_pre_mlp[l], g_post_mlp[l], w_in[l], conv_w[l],
                          conv_b[l], dt_bias[l], a_log[l], d_skip[l], g_ssd_norm[l], w_ssd_out[l],
                          w_pool_group[l], pool_scale[l], w_o[l], w_up[l], w_down[l])
        mod = _adaln(jnp.concatenate([c_prompt, c_sample], axis=0), w_ada[l], b_ada[l])
        mod_p = mod[:bp].reshape(bp, 6, d)
        mod_s = mod[bp:].reshape(bs, 6, d)
        zc = jnp.zeros((1,) + state_conv.shape[2:], yp.dtype)
        zp = jnp.zeros((1,) + state_pool.shape[2:], yp.dtype)
        zh = jnp.zeros((1,) + state_ssm.shape[2:], yp.dtype)
        yp, h_p, c_p, p_p = _layer(yp, mod_p, zc, zp, zh, p, pos0=0)
        ys, h_s, c_s, p_s = _layer(ys, mod_s, state_conv[l], state_pool[l], state_ssm[l], p, pos0=PAST_LEN)
        for lst, val in zip(outs, (h_p, c_p, p_p, h_s, c_s, p_s)):
            lst.append(val)
    return (yp, ys) + tuple(jnp.stack(o) for o in outs)
```

```python
import functools

import jax
import jax.numpy as jnp
from jax import lax
from jax.experimental import pallas as pl
from jax.experimental.pallas import tpu as pltpu

F32 = jnp.float32
BF16 = jnp.bfloat16

EPS = 1e-6
LOG2E = 1.4426950408889634
PAST_LEN = 2048
POOL_WINDOWS = (2, 4, 8, 16)
POOL_HIST = max(POOL_WINDOWS) - 1
CONV_WIDTH = 4
SSD_HEAD_DIM = 64
SSD_STATE = 128
LANES = 128
SUBLANES = 8
SSD_CHUNK = 128
ROW_GROUPS = SSD_CHUNK // SUBLANES
GROUP_COLS = 256
HEADS_PER_GROUP = GROUP_COLS // SSD_HEAD_DIM
PROJ_COLS = 512
CONV_COLS = SSD_STATE
CONV_CARRY = (CONV_WIDTH - 1) * SUBLANES
POOL_CARRY = POOL_HIST * SUBLANES
MIX_TILE = 512
MERGE_TILE = 1024
MLP_TILE = 1024
VMEM_LIMIT = 60 * 1024 * 1024


def _row_of_token(tok):
    return (tok % ROW_GROUPS) * SUBLANES + tok // ROW_GROUPS


def _tile_row(tok):
    return (tok // SSD_CHUNK) * SSD_CHUNK + _row_of_token(tok % SSD_CHUNK)


def _tokens(shape, axis):
    r = lax.broadcasted_iota(jnp.int32, shape, axis)
    return (r // SSD_CHUNK) * SSD_CHUNK + (r % SUBLANES) * ROW_GROUPS + (r % SSD_CHUNK) // SUBLANES


def _dot(a, b):
    return jnp.dot(a, b, preferred_element_type=F32)


def _dot_wt(a, w_t):
    return lax.dot_general(a, w_t, (((1,), (1,)), ((), ())), preferred_element_type=F32)


def _sigmoid(v):
    return 1.0 / (1.0 + jnp.exp2(v * (-LOG2E)))


def _silu(v):
    return v * _sigmoid(v)


def _silu_of_twice(h):
    return h + h * jnp.tanh(h)


def _rms(v, g):
    return v * lax.rsqrt(jnp.mean(v * v, axis=-1, keepdims=True) + EPS) * g


def _shifted_groups(prev_tail, cur, n_back, packed):
    if packed:
        return [prev_tail[j * SUBLANES:(j + 1) * SUBLANES] for j in range(n_back)]
    first = lax.broadcasted_iota(jnp.int32, (SUBLANES, cur.shape[1]), 0) == 0
    tail = cur[SSD_CHUNK - n_back * SUBLANES:]
    parts = []
    for j in range(n_back):
        rs = slice(j * SUBLANES, (j + 1) * SUBLANES)
        parts.append(jnp.where(first, pltpu.roll(prev_tail[rs], 1, axis=0), pltpu.roll(tail[rs], 1, axis=0)))
    return parts


def _adaln_kernel(c_ref, w_ref, b_ref, o_ref):
    c = c_ref[...]
    o_ref[...] = _dot(_silu(c).astype(BF16), w_ref[...].astype(BF16)) + b_ref[...]


def _adaln(c, w_ada, b_ada):
    rows, d = c.shape
    n = w_ada.shape[1]
    bn = d
    return pl.pallas_call(
        _adaln_kernel,
        out_shape=jax.ShapeDtypeStruct((rows, n), F32),
        grid=(n // bn,),
        in_specs=[pl.BlockSpec((rows, d), lambda j: (0, 0)),
                  pl.BlockSpec((d, bn), lambda j: (0, j)),
                  pl.BlockSpec((1, bn), lambda j: (0, j))],
        out_specs=pl.BlockSpec((rows, bn), lambda j: (0, j)),
        compiler_params=pltpu.CompilerParams(dimension_semantics=("arbitrary",)),
        name="adaln",
    )(c, w_ada, b_ada.reshape(1, n))


def _ssd_kernel(x_ref, mod_ref, convh_ref, h0_ref,
                gpre_ref, win_ref, convw_ref, convb_ref, dtb_ref, alog_ref, dskip_ref,
                gssd_ref, e2_ref,
                u_ref, yn_ref, hout_ref, convo_ref,
                perm_scr, convc_scr, xr_scr, bmt_scr, cm_scr, acs_scr, acst_scr,
                ea_scr, dte_scr, y_scr, st_scr, cb_scr, rcat_scr, mcat_scr,
                *, lt, lv, d_inner, n_groups, nseq):
    q = SSD_CHUNK
    n_chunks = lt // q
    d_bc = n_groups * SSD_STATE
    d_model = x_ref.shape[-1]
    n_slabs = d_model // LANES
    packed = nseq > 1

    def prenorm(dst_ref):
        u = _rms(x_ref[0], gpre_ref[...] * (1.0 + mod_ref[0, 1])) + mod_ref[0, 0]
        for c in range(n_chunks):
            for i in range(ROW_GROUPS):
                start = c * q + (i // 2 if i % 2 == 0 else q // 2 + i // 2)
                r0 = c * q + i * SUBLANES
                for j in range(n_slabs):
                    perm_scr[j, pl.ds(start, SUBLANES, stride=SUBLANES), :] = (
                        u[r0:r0 + SUBLANES, j * LANES:(j + 1) * LANES])
        for j in range(n_slabs):
            dst_ref[:, j * LANES:(j + 1) * LANES] = perm_scr[j].astype(BF16)

    def ssd_tile(t):
        last_t = pl.num_programs(1) - 1

        @pl.when(t == 0)
        def _init():
            convc_scr[...] = convh_ref[0]
            rcat_scr[...] = jnp.zeros(rcat_scr.shape, BF16)
            for s in range(nseq):
                for g in range(n_groups):
                    cs = slice(g * GROUP_COLS, (g + 1) * GROUP_COLS)
                    st_scr[s, :, cs] = h0_ref[s, cs, :].T

        prenorm(u_ref.at[0])

        lane = lax.broadcasted_iota(jnp.int32, (1, LANES), 1)
        n_heads = d_inner // SSD_HEAD_DIM
        r_dt = d_inner + d_inner + 2 * d_bc
        dt = jax.nn.softplus(_dot_wt(u_ref[0], win_ref[r_dt:r_dt + LANES, :]) + dtb_ref[...])
        if lv < lt:
            dt = jnp.where(_tokens((lt, LANES), 0) < lv, dt, 0.0)
        a_row = jnp.where(lane < n_heads, -jnp.exp(alog_ref[...]), 0.0)
        da = dt * a_row
        tok_l, tok_s = _tokens((q, q), 0), _tokens((q, q), 1)
        causal = tok_l >= tok_s
        if packed:
            causal = causal & (tok_l // ROW_GROUPS == tok_s // ROW_GROUPS)
        tri3 = jnp.concatenate([jnp.where(causal, 1.0, 0.0).astype(BF16)] * 3, axis=1)
        ea_parts, dte_parts = [], []
        for c in range(n_chunks):
            da_c = da[c * q:(c + 1) * q]
            hi = da_c.astype(BF16)
            rest = da_c - hi.astype(F32)
            mid = rest.astype(BF16)
            lo = (rest - mid.astype(F32)).astype(BF16)
            acs = _dot(tri3, jnp.concatenate([hi, mid, lo], axis=0))
            acs_scr[c * q:(c + 1) * q, :] = acs * LOG2E
            acst_scr[c] = (acs * LOG2E).T
            ea_parts.append(jnp.exp(acs))
            last = jnp.concatenate([acs[q - SUBLANES:]] * ROW_GROUPS, axis=0) if packed else acs[q - 1:q, :]
            dte_parts.append(jnp.exp(last - acs))

        def split_hi_lo(v):
            hi = v.astype(BF16)
            lo = (v - hi.astype(F32)).astype(BF16)
            return jnp.concatenate([hi, lo], axis=1)

        dt2 = split_hi_lo(dt)
        ea2 = split_hi_lo(jnp.concatenate(ea_parts, axis=0))
        dte2 = split_hi_lo(jnp.concatenate(dte_parts, axis=0))
        for c0 in range(0, d_inner, PROJ_COLS):
            cs = slice(c0, c0 + PROJ_COLS)
            ea_scr[:, cs] = _dot(ea2, e2_ref[:, cs])
            dte_scr[:, cs] = _dot(dte2, e2_ref[:, cs])

        for p0 in range(0, d_inner + 2 * d_bc, PROJ_COLS):
            ps = slice(p0, p0 + PROJ_COLS)
            res_wide = _dot_wt(u_ref[0], win_ref[d_inner + p0:d_inner + p0 + PROJ_COLS, :])
            dt_wide = _dot(dt2, e2_ref[:, ps]) if p0 < d_inner else None
            for h0 in range(0, PROJ_COLS, CONV_COLS):
                c0 = p0 + h0
                cs = slice(c0, c0 + CONV_COLS)
                res = res_wide[:, h0:h0 + CONV_COLS]
                prev_tail = convc_scr[:, cs]
                for c in range(n_chunks):
                    rows = slice(c * q, (c + 1) * q)
                    cur = res[rows]
                    ext = jnp.concatenate(_shifted_groups(prev_tail, cur, CONV_WIDTH - 1, packed) + [cur], axis=0)
                    acc = convb_ref[:, cs] + ext[CONV_CARRY:] * convw_ref[CONV_WIDTH - 1:CONV_WIDTH, cs]
                    for k in range(1, CONV_WIDTH):
                        lo = CONV_CARRY - k * SUBLANES
                        acc = acc + ext[lo:lo + q] * convw_ref[CONV_WIDTH - 1 - k:CONV_WIDTH - k, cs]
                    v = _silu_of_twice(acc)
                    if c0 < d_inner:
                        y_scr[rows, cs] = v * dskip_ref[:, cs]
                        xr_scr[rows, cs] = v * dt_wide[rows, h0:h0 + CONV_COLS]
                    elif c0 < d_inner + d_bc:
                        b0 = c0 - d_inner
                        bmt_scr[c * n_groups + b0 // SSD_STATE] = v.T.astype(BF16)
                    else:
                        b0 = c0 - d_inner - d_bc
                        cm_scr[rows, b0:b0 + CONV_COLS] = v.astype(BF16)
                    prev_tail = cur[q - CONV_CARRY:]
                convc_scr[:, cs] = prev_tail
                if packed:
                    convo_ref[0, :, cs] = res[q - CONV_CARRY:]
                else:
                    for j in range(CONV_WIDTH - 1):
                        r = _tile_row(lv - (CONV_WIDTH - 1) + j)
                        convo_ref[0, j:j + 1, cs] = res[r:r + 1]

        if packed:
            row_seq = lax.broadcasted_iota(jnp.int32, (q, SSD_STATE), 0) % SUBLANES
            col_seq = lax.broadcasted_iota(jnp.int32, (SSD_STATE, q), 1) % SUBLANES
            row_masks = [jnp.where(row_seq == s, 1.0, 0.0).astype(BF16) for s in range(nseq)]
            col_masks = [jnp.where(col_seq == s, 1.0, 0.0).astype(BF16) for s in range(nseq)]

        def chunk_body(c, carry):
            r0 = pl.multiple_of(c * q, q)
            crow = pl.ds(r0, q)
            for g in range(n_groups):
                gs = slice(g * GROUP_COLS, (g + 1) * GROUP_COLS)
                ns = slice(g * SSD_STATE, (g + 1) * SSD_STATE)
                cb_scr[g] = _dot(cm_scr[crow, ns], bmt_scr[c * n_groups + g])
                xr_bf = xr_scr[crow, gs].astype(BF16)
                for j in range(HEADS_PER_GROUP):
                    ls = slice(j * SSD_HEAD_DIM, (j + 1) * SSD_HEAD_DIM)
                    rcat_scr[g, j * q:(j + 1) * q, ls] = xr_bf[:, ls]
            acs = acs_scr[crow, :]
            acst = acst_scr[c]
            for g in range(n_groups):
                cb = cb_scr[g]
                for j in range(HEADS_PER_GROUP):
                    h = g * HEADS_PER_GROUP + j
                    seg = acs[:, h:h + 1] - acst[h:h + 1, :]
                    dec = jnp.exp2(jnp.where(causal, seg, -jnp.inf))
                    mcat_scr[g, :, j * q:(j + 1) * q] = (cb * dec).astype(BF16)
            for g in range(n_groups):
                gs = slice(g * GROUP_COLS, (g + 1) * GROUP_COLS)
                ns = slice(g * SSD_STATE, (g + 1) * SSD_STATE)
                y_diag = _dot(mcat_scr[g], rcat_scr[g])
                xrd = (xr_scr[crow, gs] * dte_scr[crow, gs]).astype(BF16)
                if packed:
                    cg = cm_scr[crow, ns]
                    c_cat = jnp.concatenate([cg * row_masks[s] for s in range(nseq)], axis=1)
                    st_cat = jnp.concatenate([st_scr[s, :, gs].astype(BF16) for s in range(nseq)], axis=0)
                    y_off = _dot(c_cat, st_cat) * ea_scr[crow, gs]
                    bt = bmt_scr[c * n_groups + g]
                    for s in range(nseq):
                        cd = ea_scr[q - SUBLANES + s:q - SUBLANES + s + 1, gs]
                        st_scr[s, :, gs] = st_scr[s, :, gs] * cd + _dot(bt * col_masks[s], xrd)
                else:
                    st = st_scr[0, :, gs]
                    y_off = _dot(cm_scr[crow, ns], st.astype(BF16)) * ea_scr[crow, gs]
                    cd = ea_scr[pl.ds(r0 + q - 1, 1), gs]
                    st_scr[0, :, gs] = st * cd + _dot(bmt_scr[c * n_groups + g], xrd)
                y_scr[crow, gs] = y_scr[crow, gs] + y_diag + y_off
            return carry

        lax.fori_loop(0, n_chunks, chunk_body, 0)

        @pl.when(t == last_t)
        def _ssm_state():
            for s in range(nseq):
                for g in range(n_groups):
                    cs = slice(g * GROUP_COLS, (g + 1) * GROUP_COLS)
                    hout_ref[s, cs, :] = st_scr[s, :, cs].T

        for p0 in range(0, d_inner, PROJ_COLS):
            z_wide = _dot_wt(u_ref[0], win_ref[p0:p0 + PROJ_COLS, :])
            for h0 in range(0, PROJ_COLS, GROUP_COLS):
                gs = slice(p0 + h0, p0 + h0 + GROUP_COLS)
                yn_ref[0, :, gs] = _rms(y_scr[:, gs] * _silu_of_twice(z_wide[:, h0:h0 + GROUP_COLS]),
                                        gssd_ref[:, gs]).astype(BF16)

    ssd_tile(pl.program_id(1))


def _ssd(x, mod, conv_hist, h0, p, *, lt, lv, nseq):
    bsz, lpad, d = x.shape
    nt = lpad // lt
    d_inner = p["dskip"].shape[1]
    conv_dim = p["convw"].shape[1]
    n_groups = (conv_dim - d_inner) // (2 * SSD_STATE)
    n_chunks = lt // SSD_CHUNK
    hp = h0.shape[1]
    mod_rows = mod.shape[2]
    conv_rows = (CONV_WIDTH - 1) * nseq
    assert lv >= CONV_WIDTH - 1 and lt % SSD_CHUNK == 0 and d_inner == n_groups * GROUP_COLS
    assert nseq == 1 or (nseq == SUBLANES and lt == lv == lpad == SSD_CHUNK)

    weights = [p["gpre"], p["win_ssd"], p["convw"], p["convb"], p["dtb"], p["alog"],
               p["dskip"], p["gssd"], p["e2"]]
    tile_map = lambda b, t: (b, t, 0)
    seq_map = lambda b, t: (b, 0, 0)
    shared_map = lambda b, t: (0, 0, 0)
    in_specs = [
        pl.BlockSpec((1, lt, d), tile_map),
        pl.BlockSpec((1, 6, mod_rows, d), lambda b, t: (b, 0, 0, 0)),
        pl.BlockSpec((1, CONV_CARRY, conv_dim), seq_map if conv_hist.shape[0] == bsz else shared_map),
        pl.BlockSpec((nseq, hp, SSD_STATE), seq_map if h0.shape[0] == bsz * nseq else shared_map),
    ] + [pl.BlockSpec(memory_space=pltpu.VMEM) for _ in weights]
    out_shape = (
        jax.ShapeDtypeStruct((bsz, lpad, d), BF16),
        jax.ShapeDtypeStruct((bsz, lpad, d_inner), BF16),
        jax.ShapeDtypeStruct((bsz * nseq, hp, SSD_STATE), F32),
        jax.ShapeDtypeStruct((bsz, conv_rows, conv_dim), F32),
    )
    out_specs = (
        pl.BlockSpec((1, lt, d), tile_map),
        pl.BlockSpec((1, lt, d_inner), tile_map),
        pl.BlockSpec((nseq, hp, SSD_STATE), seq_map),
        pl.BlockSpec((1, conv_rows, conv_dim), seq_map),
    )
    scratch = [
        pltpu.VMEM((d // LANES, lt, LANES), F32),
        pltpu.VMEM((CONV_CARRY, conv_dim), F32),
        pltpu.VMEM((lt, d_inner), F32),
        pltpu.VMEM((n_chunks * n_groups, SSD_STATE, SSD_CHUNK), BF16),
        pltpu.VMEM((lt, n_groups * SSD_STATE), BF16),
        pltpu.VMEM((lt, LANES), F32),
        pltpu.VMEM((n_chunks, LANES, SSD_CHUNK), F32),
        pltpu.VMEM((lt, d_inner), F32),
        pltpu.VMEM((lt, d_inner), F32),
        pltpu.VMEM((lt, d_inner), F32),
        pltpu.VMEM((nseq, SSD_STATE, hp), F32),
        pltpu.VMEM((n_groups, SSD_CHUNK, SSD_CHUNK), F32),
        pltpu.VMEM((n_groups, HEADS_PER_GROUP * SSD_CHUNK, GROUP_COLS), BF16),
        pltpu.VMEM((n_groups, SSD_CHUNK, HEADS_PER_GROUP * SSD_CHUNK), BF16),
    ]
    kern = functools.partial(_ssd_kernel, lt=lt, lv=lv, d_inner=d_inner, n_groups=n_groups, nseq=nseq)
    return pl.pallas_call(
        kern, out_shape=out_shape, grid=(bsz, nt), in_specs=in_specs, out_specs=out_specs,
        scratch_shapes=scratch,
        compiler_params=pltpu.CompilerParams(dimension_semantics=("arbitrary", "arbitrary"),
                                             vmem_limit_bytes=VMEM_LIMIT),
        name="ssd",
    )(x, mod, conv_hist, h0, *weights)


def _merge_kernel(x_ref, u_ref, yn_ref, mod_ref, poolh_ref,
                  gpost_ref, win_ref, wssd_ref, wpool_ref, pscale_ref, wo_ref,
                  out_ref, poolo_ref,
                  perm_scr, poolc_scr, ypool_scr,
                  *, lt, lv, pos0, nseq):
    q = SSD_CHUNK
    n_chunks = lt // q
    t = pl.program_id(1)
    packed = nseq > 1
    d_model = x_ref.shape[-1]
    n_slabs = d_model // LANES

    @pl.when(t == 0)
    def _init():
        poolc_scr[...] = poolh_ref[0]

    d_pool = ypool_scr.shape[1]
    p_all = _dot_wt(u_ref[0], win_ref[0:d_pool, :])
    for gi, w in enumerate(POOL_WINDOWS):
        gs = slice(gi * GROUP_COLS, (gi + 1) * GROUP_COLS)
        res = p_all[:, gs]
        prev_tail = poolc_scr[:, gs]
        for c in range(n_chunks):
            rows = slice(c * q, (c + 1) * q)
            cur = res[rows]
            back = _shifted_groups(prev_tail[POOL_CARRY - (w - 1) * SUBLANES:], cur, w - 1, packed)
            win = jnp.concatenate(back + [cur], axis=0)
            span = 1
            while span < w:
                win = win[span * SUBLANES:] + win[:win.shape[0] - span * SUBLANES]
                span *= 2
            if pos0 >= POOL_HIST or c > 0:
                pm = win * (1.0 / w) - cur
            else:
                cnt = jnp.minimum(_tokens((q, GROUP_COLS), 0) + (pos0 + 1), w).astype(F32)
                inv = jnp.where(t == 0, 1.0 / cnt, 1.0 / w)
                pm = win * inv - cur
            ypool_scr[rows, gs] = _dot(pm.astype(BF16), wpool_ref[gi]) * pscale_ref[:, gs]
            prev_tail = cur[SUBLANES:]
        poolc_scr[:, gs] = prev_tail
        if packed:
            poolo_ref[0, :, gs] = res[q - POOL_CARRY:]
        else:
            for j in range(POOL_HIST):
                r = _tile_row(lv - POOL_HIST + j)
                poolo_ref[0, j:j + 1, gs] = res[r:r + 1]

    y_ssd = _dot(yn_ref[0], wssd_ref[...])
    gate_ssd = _sigmoid(_dot_wt(u_ref[0], win_ref[d_pool:d_pool + d_model, :]))
    gate_pool = _sigmoid(_dot_wt(u_ref[0], win_ref[d_pool + d_model:d_pool + 2 * d_model, :]))
    merged = (gate_ssd * y_ssd + gate_pool * ypool_scr[...]).astype(BF16)
    mix = _dot(merged, wo_ref[...])
    branch = mod_ref[0, 2] * _rms(mix, gpost_ref[...])
    for j in range(n_slabs):
        perm_scr[j] = branch[:, j * LANES:(j + 1) * LANES]
    for c in range(n_chunks):
        for i in range(ROW_GROUPS):
            start = c * q + (i // 2 if i % 2 == 0 else q // 2 + i // 2)
            rows = slice(c * q + i * SUBLANES, c * q + (i + 1) * SUBLANES)
            for j in range(n_slabs):
                ls = slice(j * LANES, (j + 1) * LANES)
                out_ref[0, rows, ls] = x_ref[0, rows, ls] + perm_scr[j, pl.ds(start, SUBLANES, stride=SUBLANES), :]


def _merge(x, u, yn, mod, pool_hist, p, *, lt, lv, pos0, nseq):
    bsz, lpad, d = x.shape
    nt = lpad // lt
    d_inner = yn.shape[-1]
    d_pool = p["pscale"].shape[1]
    mod_rows = mod.shape[2]
    pool_rows = POOL_HIST * nseq
    assert lv >= POOL_HIST and d_pool == len(POOL_WINDOWS) * GROUP_COLS
    assert nseq == 1 or (nseq == SUBLANES and lt == lv == lpad == SSD_CHUNK and pos0 >= POOL_HIST)
    weights = [p["gpost_mix"], p["win_merge"], p["wssd"], p["wpool"], p["pscale"], p["wo"]]
    in_specs = [
        pl.BlockSpec((1, lt, d), lambda b, t: (b, t, 0)),
        pl.BlockSpec((1, lt, d), lambda b, t: (b, t, 0)),
        pl.BlockSpec((1, lt, d_inner), lambda b, t: (b, t, 0)),
        pl.BlockSpec((1, 6, mod_rows, d), lambda b, t: (b, 0, 0, 0)),
        pl.BlockSpec((1, POOL_CARRY, d_pool), lambda b, t: (b if pool_hist.shape[0] == bsz else 0, 0, 0)),
    ] + [pl.BlockSpec(memory_space=pltpu.VMEM) for _ in weights]
    out_shape = (
        jax.ShapeDtypeStruct((bsz, lpad, d), F32),
        jax.ShapeDtypeStruct((bsz, pool_rows, d_pool), F32),
    )
    out_specs = (
        pl.BlockSpec((1, lt, d), lambda b, t: (b, t, 0)),
        pl.BlockSpec((1, pool_rows, d_pool), lambda b, t: (b, 0, 0)),
    )
    scratch = [
        pltpu.VMEM((d // LANES, lt, LANES), F32),
        pltpu.VMEM((POOL_CARRY, d_pool), F32),
        pltpu.VMEM((lt, d_pool), F32),
    ]
    kern = functools.partial(_merge_kernel, lt=lt, lv=lv, pos0=pos0, nseq=nseq)
    return pl.pallas_call(
        kern, out_shape=out_shape, grid=(bsz, nt), in_specs=in_specs, out_specs=out_specs,
        scratch_shapes=scratch,
        compiler_params=pltpu.CompilerParams(dimension_semantics=("arbitrary", "arbitrary"),
                                             vmem_limit_bytes=VMEM_LIMIT),
        name="merge",
    )(x, u, yn, mod, pool_hist, *weights)


def _mlp_kernel(x_ref, mod_ref, gpre_ref, gpost_ref, wup_ref, wdown_ref, out_ref, v_scr):
    x = x_ref[0]
    v_scr[...] = (_rms(x, gpre_ref[...] * (1.0 + mod_ref[0, 4])) + mod_ref[0, 3]).astype(BF16)
    acc = None
    for c0 in range(0, wup_ref.shape[1], PROJ_COLS):
        h = jnp.square(jnp.maximum(_dot(v_scr[...], wup_ref[:, c0:c0 + PROJ_COLS]), 0.0)).astype(BF16)
        part = _dot(h, wdown_ref[c0:c0 + PROJ_COLS, :])
        acc = part if acc is None else acc + part
    out_ref[0] = x + mod_ref[0, 5] * _rms(acc, gpost_ref[...])


def _mlp(x, mod, p, *, tm):
    bsz, lpad, d = x.shape
    nt = lpad // tm
    weights = [p["gpre_mlp"], p["gpost_mlp"], p["wup"], p["wdown"]]
    return pl.pallas_call(
        _mlp_kernel,
        out_shape=jax.ShapeDtypeStruct((bsz, lpad, d), F32),
        grid=(bsz, nt),
        in_specs=[pl.BlockSpec((1, tm, d), lambda b, t: (b, t, 0)),
                  pl.BlockSpec((1, 6, mod.shape[2], d), lambda b, t: (b, 0, 0, 0))]
                 + [pl.BlockSpec(memory_space=pltpu.VMEM) for _ in weights],
        out_specs=pl.BlockSpec((1, tm, d), lambda b, t: (b, t, 0)),
        scratch_shapes=[pltpu.VMEM((tm, d), BF16)],
        compiler_params=pltpu.CompilerParams(dimension_semantics=("arbitrary", "arbitrary"),
                                             vmem_limit_bytes=VMEM_LIMIT),
        name="mlp",
    )(x, mod, *weights)


def _prep_weights(g_pre_mix, g_post_mix, g_pre_mlp, g_post_mlp, w_in, conv_w, conv_b, dt_bias, a_log,
                  d_skip, g_ssd_norm, w_ssd_out, w_pool_group, pool_scale, w_o, w_up, w_down):
    d_inner = w_ssd_out.shape[0]
    conv_dim = conv_w.shape[1]
    n_heads = dt_bias.shape[0]
    d_pool = pool_scale.shape[0]
    s_z, s_xbc = d_inner, d_inner + conv_dim
    s_dt, s_pool = s_xbc + n_heads, s_xbc + n_heads + d_pool
    pad = LANES - n_heads
    head_of_col = jnp.arange(d_inner) // SSD_HEAD_DIM
    e1 = (jnp.arange(LANES)[:, None] == head_of_col[None, :]).astype(BF16)
    row = lambda a: a.reshape(1, -1).astype(F32)
    z_rows = (jnp.arange(w_in.shape[1]) < s_z)[:, None]
    w_bf = (jnp.swapaxes(w_in, 0, 1) * jnp.where(z_rows, 0.5, 1.0)).astype(BF16)
    return dict(
        gpre=row(g_pre_mix), gpost_mix=row(g_post_mix), gpre_mlp=row(g_pre_mlp), gpost_mlp=row(g_post_mlp),
        win_ssd=w_bf[:s_xbc + LANES], win_merge=w_bf[s_dt:],
        convw=0.5 * conv_w.astype(F32), convb=0.5 * row(conv_b),
        dtb=jnp.pad(row(dt_bias), ((0, 0), (0, pad))), alog=jnp.pad(row(a_log), ((0, 0), (0, pad))),
        dskip=jnp.repeat(row(d_skip), SSD_HEAD_DIM, axis=1), gssd=row(g_ssd_norm),
        e2=jnp.concatenate([e1, e1], axis=0),
        wssd=w_ssd_out.astype(BF16), wpool=w_pool_group.astype(BF16), pscale=row(pool_scale),
        wo=w_o.astype(BF16), wup=w_up.astype(BF16), wdown=w_down.astype(BF16),
    )


def _history_rows(hist):
    bsz, n, c = hist.shape
    out = jnp.zeros((bsz, n, SUBLANES, c), hist.dtype).at[:, :, SUBLANES - 1, :].set(hist)
    return out.reshape(bsz, n * SUBLANES, c)


def _layer(x, mod, conv_hist, pool_hist, h0, p, *, pos0):
    bsz, seq, d = x.shape
    heads, hdim, nstate = h0.shape[1:]
    h0 = h0.reshape(h0.shape[0], heads * hdim, nstate)
    if bsz == SUBLANES and seq == ROW_GROUPS and pos0 >= POOL_HIST:
        return _layer_packed(x, mod, conv_hist, pool_hist, h0, p, pos0=pos0, state_shape=(heads, hdim, nstate))
    if seq % MIX_TILE == 0:
        lt, lpad = MIX_TILE, seq
    else:
        lt = lpad = -(-seq // SSD_CHUNK) * SSD_CHUNK
    lv = seq - (lpad - lt)
    xp = x if lpad == seq else jnp.pad(x, ((0, 0), (0, lpad - seq), (0, 0)))
    mod = mod[:, :, None, :]
    u, yn, h_new, conv_state = _ssd(xp, mod, _history_rows(conv_hist), h0, p, lt=lt, lv=lv, nseq=1)
    lm = MERGE_TILE if lpad % MERGE_TILE == 0 else lt
    x1, pool_state = _merge(xp, u, yn, mod, _history_rows(pool_hist), p, lt=lm, lv=lv + lm - lt, pos0=pos0, nseq=1)
    tm = MLP_TILE if lpad % MLP_TILE == 0 else lpad
    y = _mlp(x1, mod, p, tm=tm)
    if lpad != seq:
        y = y[:, :seq]
    return y, h_new.reshape(bsz, heads, hdim, nstate), conv_state, pool_state


def _layer_packed(x, mod, conv_hist, pool_hist, h0, p, *, pos0, state_shape):
    bsz, seq, d = x.shape
    rows = bsz * seq
    by_step = lambda a: a.transpose(1, 0, 2).reshape(1, a.shape[1] * bsz, a.shape[2])
    from_step = lambda a, n: a.reshape(n, bsz, a.shape[-1]).transpose(1, 0, 2)
    mod_t = mod.transpose(1, 0, 2)
    mod_tok = jnp.repeat(mod_t, seq, axis=1)[None]
    mod_perm = jnp.tile(mod_t, (1, seq, 1))[None]
    xp = x.reshape(1, rows, d)
    u, yn, h_new, conv_state = _ssd(xp, mod_tok, by_step(conv_hist), h0, p, lt=rows, lv=rows, nseq=bsz)
    x1, pool_state = _merge(xp, u, yn, mod_perm, by_step(pool_hist), p, lt=rows, lv=rows, pos0=pos0, nseq=bsz)
    y = _mlp(x1, mod_tok, p, tm=rows)
    return (y.reshape(bsz, seq, d), h_new.reshape((bsz,) + state_shape),
            from_step(conv_state, CONV_WIDTH - 1), from_step(pool_state, POOL_HIST))


def kernel(x_prompt, x_sample, state_ssm, state_conv, state_pool, c_prompt, c_sample, w_ada, b_ada, g_pre_mix, g_post_mix, g_pre_mlp, g_post_mlp, w_in, conv_w, conv_b, dt_bias, a_log, d_skip, g_ssd_norm, w_ssd_out, w_pool_group, pool_scale, w_o, w_up, w_down):
    yp, ys = x_prompt, x_sample
    bp, d = x_prompt.shape[0], x_prompt.shape[2]
    bs = x_sample.shape[0]
    depth = w_in.shape[0]
    outs = [[] for _ in range(6)]
    for l in range(depth):
        p = _prep_weights(g_pre_mix[l], g_post_mix[l], g_pre_mlp[l], g_post_mlp[l], w_in[l], conv_w[l],
                          conv_b[l], dt_bias[l], a_log[l], d_skip[l], g_ssd_norm[l], w_ssd_out[l],
                          w_pool_group[l], pool_scale[l], w_o[l], w_up[l], w_down[l])
        mod = _adaln(jnp.concatenate([c_prompt, c_sample], axis=0), w_ada[l], b_ada[l])
        mod_p = mod[:bp].reshape(bp, 6, d)
        mod_s = mod[bp:].reshape(bs, 6, d)
        zc = jnp.zeros((1,) + state_conv.shape[2:], yp.dtype)
        zp = jnp.zeros((1,) + state_pool.shape[2:], yp.dtype)
        zh = jnp.zeros((1,) + state_ssm.shape[2:], yp.dtype)
        yp, h_p, c_p, p_p = _layer(yp, mod_p, zc, zp, zh, p, pos0=0)
        ys, h_s, c_s, p_s = _layer(ys, mod_s, state_conv[l], state_pool[l], state_ssm[l], p, pos0=PAST_LEN)
        for lst, val in zip(outs, (h_p, c_p, p_p, h_s, c_s, p_s)):
            lst.append(val)
    return (yp, ys) + tuple(jnp.stack(o) for o in outs)
```

```python
import functools

import jax
import jax.numpy as jnp
from jax import lax
from jax.experimental import pallas as pl
from jax.experimental.pallas import tpu as pltpu

F32 = jnp.float32
BF16 = jnp.bfloat16

EPS = 1e-6
LOG2E = 1.4426950408889634
PAST_LEN = 2048
POOL_WINDOWS = (2, 4, 8, 16)
POOL_HIST = max(POOL_WINDOWS) - 1
CONV_WIDTH = 4
SSD_HEAD_DIM = 64
SSD_STATE = 128
LANES = 128
SUBLANES = 8
SSD_CHUNK = 128
ROW_GROUPS = SSD_CHUNK // SUBLANES
GROUP_COLS = 256
HEADS_PER_GROUP = GROUP_COLS // SSD_HEAD_DIM
PROJ_COLS = 2048
CONV_COLS = SSD_STATE
CONV_CARRY = (CONV_WIDTH - 1) * SUBLANES
POOL_CARRY = POOL_HIST * SUBLANES
MIX_TILE = 512
MERGE_TILE = 1024
MLP_TILE = 1024
VMEM_LIMIT = 60 * 1024 * 1024


def _row_of_token(tok):
    return (tok % ROW_GROUPS) * SUBLANES + tok // ROW_GROUPS


def _tile_row(tok):
    return (tok // SSD_CHUNK) * SSD_CHUNK + _row_of_token(tok % SSD_CHUNK)


def _tokens(shape, axis):
    r = lax.broadcasted_iota(jnp.int32, shape, axis)
    return (r // SSD_CHUNK) * SSD_CHUNK + (r % SUBLANES) * ROW_GROUPS + (r % SSD_CHUNK) // SUBLANES


def _dot(a, b):
    return jnp.dot(a, b, preferred_element_type=F32)


def _dot_wt(a, w_t):
    return lax.dot_general(a, w_t, (((1,), (1,)), ((), ())), preferred_element_type=F32)


def _sigmoid(v):
    return 1.0 / (1.0 + jnp.exp2(v * (-LOG2E)))


def _silu(v):
    return v * _sigmoid(v)


def _silu_of_twice(h):
    return h + h * jnp.tanh(h)


def _rms(v, g):
    return v * lax.rsqrt(jnp.mean(v * v, axis=-1, keepdims=True) + EPS) * g


def _shifted_groups(prev_tail, cur, n_back, packed):
    if packed:
        return [prev_tail[j * SUBLANES:(j + 1) * SUBLANES] for j in range(n_back)]
    first = lax.broadcasted_iota(jnp.int32, (SUBLANES, cur.shape[1]), 0) == 0
    tail = cur[SSD_CHUNK - n_back * SUBLANES:]
    parts = []
    for j in range(n_back):
        rs = slice(j * SUBLANES, (j + 1) * SUBLANES)
        parts.append(jnp.where(first, pltpu.roll(prev_tail[rs], 1, axis=0), pltpu.roll(tail[rs], 1, axis=0)))
    return parts


def _adaln_kernel(c_ref, w_ref, b_ref, o_ref):
    c = c_ref[...]
    o_ref[...] = _dot(_silu(c).astype(BF16), w_ref[...].astype(BF16)) + b_ref[...]


def _adaln(c, w_ada, b_ada):
    rows, d = c.shape
    n = w_ada.shape[1]
    bn = d
    return pl.pallas_call(
        _adaln_kernel,
        out_shape=jax.ShapeDtypeStruct((rows, n), F32),
        grid=(n // bn,),
        in_specs=[pl.BlockSpec((rows, d), lambda j: (0, 0)),
                  pl.BlockSpec((d, bn), lambda j: (0, j)),
                  pl.BlockSpec((1, bn), lambda j: (0, j))],
        out_specs=pl.BlockSpec((rows, bn), lambda j: (0, j)),
        compiler_params=pltpu.CompilerParams(dimension_semantics=("arbitrary",)),
        name="adaln",
    )(c, w_ada, b_ada.reshape(1, n))


def _ssd_kernel(x_ref, mod_ref, convh_ref, h0_ref,
                gpre_ref, win_ref, convw_ref, convb_ref, dtb_ref, alog_ref, dskip_ref,
                gssd_ref, e2_ref,
                u_ref, yn_ref, hout_ref, convo_ref,
                perm_scr, convc_scr, xr_scr, bmt_scr, cm_scr, acs_scr, acst_scr,
                ea_scr, dte_scr, y_scr, st_scr, cb_scr, rcat_scr, mcat_scr,
                *, lt, lv, d_inner, n_groups, nseq):
    q = SSD_CHUNK
    n_chunks = lt // q
    d_bc = n_groups * SSD_STATE
    d_model = x_ref.shape[-1]
    n_slabs = d_model // LANES
    packed = nseq > 1

    def prenorm(dst_ref):
        u = _rms(x_ref[0], gpre_ref[...] * (1.0 + mod_ref[0, 1])) + mod_ref[0, 0]
        for c in range(n_chunks):
            for i in range(ROW_GROUPS):
                start = c * q + (i // 2 if i % 2 == 0 else q // 2 + i // 2)
                r0 = c * q + i * SUBLANES
                for j in range(n_slabs):
                    perm_scr[j, pl.ds(start, SUBLANES, stride=SUBLANES), :] = (
                        u[r0:r0 + SUBLANES, j * LANES:(j + 1) * LANES])
        for j in range(n_slabs):
            dst_ref[:, j * LANES:(j + 1) * LANES] = perm_scr[j].astype(BF16)

    def ssd_tile(t):
        last_t = pl.num_programs(1) - 1

        @pl.when(t == 0)
        def _init():
            convc_scr[...] = convh_ref[0]
            rcat_scr[...] = jnp.zeros(rcat_scr.shape, BF16)
            for s in range(nseq):
                for g in range(n_groups):
                    cs = slice(g * GROUP_COLS, (g + 1) * GROUP_COLS)
                    st_scr[s, :, cs] = h0_ref[s, cs, :].T

        prenorm(u_ref.at[0])

        lane = lax.broadcasted_iota(jnp.int32, (1, LANES), 1)
        n_heads = d_inner // SSD_HEAD_DIM
        r_dt = d_inner + d_inner + 2 * d_bc
        dt = jax.nn.softplus(_dot_wt(u_ref[0], win_ref[r_dt:r_dt + LANES, :]) + dtb_ref[...])
        if lv < lt:
            dt = jnp.where(_tokens((lt, LANES), 0) < lv, dt, 0.0)
        a_row = jnp.where(lane < n_heads, -jnp.exp(alog_ref[...]), 0.0)
        da = dt * a_row
        tok_l, tok_s = _tokens((q, q), 0), _tokens((q, q), 1)
        causal = tok_l >= tok_s
        if packed:
            causal = causal & (tok_l // ROW_GROUPS == tok_s // ROW_GROUPS)
        tri3 = jnp.concatenate([jnp.where(causal, 1.0, 0.0).astype(BF16)] * 3, axis=1)
        ea_parts, dte_parts = [], []
        for c in range(n_chunks):
            da_c = da[c * q:(c + 1) * q]
            hi = da_c.astype(BF16)
            rest = da_c - hi.astype(F32)
            mid = rest.astype(BF16)
            lo = (rest - mid.astype(F32)).astype(BF16)
            acs = _dot(tri3, jnp.concatenate([hi, mid, lo], axis=0))
            acs_scr[c * q:(c + 1) * q, :] = acs * LOG2E
            acst_scr[c] = (acs * LOG2E).T
            ea_parts.append(jnp.exp(acs))
            last = jnp.concatenate([acs[q - SUBLANES:]] * ROW_GROUPS, axis=0) if packed else acs[q - 1:q, :]
            dte_parts.append(jnp.exp(last - acs))

        def split_hi_lo(v):
            hi = v.astype(BF16)
            lo = (v - hi.astype(F32)).astype(BF16)
            return jnp.concatenate([hi, lo], axis=1)

        dt2 = split_hi_lo(dt)
        ea2 = split_hi_lo(jnp.concatenate(ea_parts, axis=0))
        dte2 = split_hi_lo(jnp.concatenate(dte_parts, axis=0))
        for c0 in range(0, d_inner, PROJ_COLS):
            cs = slice(c0, c0 + PROJ_COLS)
            ea_scr[:, cs] = _dot(ea2, e2_ref[:, cs])
            dte_scr[:, cs] = _dot(dte2, e2_ref[:, cs])

        for p0 in range(0, d_inner + 2 * d_bc, PROJ_COLS):
            ps = slice(p0, p0 + PROJ_COLS)
            res_wide = _dot_wt(u_ref[0], win_ref[d_inner + p0:d_inner + p0 + PROJ_COLS, :])
            dt_wide = _dot(dt2, e2_ref[:, ps]) if p0 < d_inner else None
            for h0 in range(0, PROJ_COLS, CONV_COLS):
                c0 = p0 + h0
                cs = slice(c0, c0 + CONV_COLS)
                res = res_wide[:, h0:h0 + CONV_COLS]
                prev_tail = convc_scr[:, cs]
                for c in range(n_chunks):
                    rows = slice(c * q, (c + 1) * q)
                    cur = res[rows]
                    ext = jnp.concatenate(_shifted_groups(prev_tail, cur, CONV_WIDTH - 1, packed) + [cur], axis=0)
                    acc = convb_ref[:, cs] + ext[CONV_CARRY:] * convw_ref[CONV_WIDTH - 1:CONV_WIDTH, cs]
                    for k in range(1, CONV_WIDTH):
                        lo = CONV_CARRY - k * SUBLANES
                        acc = acc + ext[lo:lo + q] * convw_ref[CONV_WIDTH - 1 - k:CONV_WIDTH - k, cs]
                    v = _silu_of_twice(acc)
                    if c0 < d_inner:
                        y_scr[rows, cs] = v * dskip_ref[:, cs]
                        xr_scr[rows, cs] = v * dt_wide[rows, h0:h0 + CONV_COLS]
                    elif c0 < d_inner + d_bc:
                        b0 = c0 - d_inner
                        bmt_scr[c * n_groups + b0 // SSD_STATE] = v.T.astype(BF16)
                    else:
                        b0 = c0 - d_inner - d_bc
                        cm_scr[rows, b0:b0 + CONV_COLS] = v.astype(BF16)
                    prev_tail = cur[q - CONV_CARRY:]
                convc_scr[:, cs] = prev_tail
                if packed:
                    convo_ref[0, :, cs] = res[q - CONV_CARRY:]
                else:
                    for j in range(CONV_WIDTH - 1):
                        r = _tile_row(lv - (CONV_WIDTH - 1) + j)
                        convo_ref[0, j:j + 1, cs] = res[r:r + 1]

        if packed:
            row_seq = lax.broadcasted_iota(jnp.int32, (q, SSD_STATE), 0) % SUBLANES
            col_seq = lax.broadcasted_iota(jnp.int32, (SSD_STATE, q), 1) % SUBLANES
            row_masks = [jnp.where(row_seq == s, 1.0, 0.0).astype(BF16) for s in range(nseq)]
            col_masks = [jnp.where(col_seq == s, 1.0, 0.0).astype(BF16) for s in range(nseq)]

        def chunk_body(c, carry):
            r0 = pl.multiple_of(c * q, q)
            crow = pl.ds(r0, q)
            for g in range(n_groups):
                gs = slice(g * GROUP_COLS, (g + 1) * GROUP_COLS)
                ns = slice(g * SSD_STATE, (g + 1) * SSD_STATE)
                cb_scr[g] = _dot(cm_scr[crow, ns], bmt_scr[c * n_groups + g])
                xr_bf = xr_scr[crow, gs].astype(BF16)
                for j in range(HEADS_PER_GROUP):
                    ls = slice(j * SSD_HEAD_DIM, (j + 1) * SSD_HEAD_DIM)
                    rcat_scr[g, j * q:(j + 1) * q, ls] = xr_bf[:, ls]
            acs = acs_scr[crow, :]
            acst = acst_scr[c]
            for g in range(n_groups):
                cb = cb_scr[g]
                for j in range(HEADS_PER_GROUP):
                    h = g * HEADS_PER_GROUP + j
                    seg = acs[:, h:h + 1] - acst[h:h + 1, :]
                    dec = jnp.exp2(jnp.where(causal, seg, -jnp.inf))
                    mcat_scr[g, :, j * q:(j + 1) * q] = (cb * dec).astype(BF16)
            for g in range(n_groups):
                gs = slice(g * GROUP_COLS, (g + 1) * GROUP_COLS)
                ns = slice(g * SSD_STATE, (g + 1) * SSD_STATE)
                y_diag = _dot(mcat_scr[g], rcat_scr[g])
                xrd = (xr_scr[crow, gs] * dte_scr[crow, gs]).astype(BF16)
                if packed:
                    cg = cm_scr[crow, ns]
                    c_cat = jnp.concatenate([cg * row_masks[s] for s in range(nseq)], axis=1)
                    st_cat = jnp.concatenate([st_scr[s, :, gs].astype(BF16) for s in range(nseq)], axis=0)
                    y_off = _dot(c_cat, st_cat) * ea_scr[crow, gs]
                    bt = bmt_scr[c * n_groups + g]
                    for s in range(nseq):
                        cd = ea_scr[q - SUBLANES + s:q - SUBLANES + s + 1, gs]
                        st_scr[s, :, gs] = st_scr[s, :, gs] * cd + _dot(bt * col_masks[s], xrd)
                else:
                    st = st_scr[0, :, gs]
                    y_off = _dot(cm_scr[crow, ns], st.astype(BF16)) * ea_scr[crow, gs]
                    cd = ea_scr[pl.ds(r0 + q - 1, 1), gs]
                    st_scr[0, :, gs] = st * cd + _dot(bmt_scr[c * n_groups + g], xrd)
                y_scr[crow, gs] = y_scr[crow, gs] + y_diag + y_off
            return carry

        lax.fori_loop(0, n_chunks, chunk_body, 0)

        @pl.when(t == last_t)
        def _ssm_state():
            for s in range(nseq):
                for g in range(n_groups):
                    cs = slice(g * GROUP_COLS, (g + 1) * GROUP_COLS)
                    hout_ref[s, cs, :] = st_scr[s, :, cs].T

        for p0 in range(0, d_inner, PROJ_COLS):
            z_wide = _dot_wt(u_ref[0], win_ref[p0:p0 + PROJ_COLS, :])
            for h0 in range(0, PROJ_COLS, GROUP_COLS):
                gs = slice(p0 + h0, p0 + h0 + GROUP_COLS)
                yn_ref[0, :, gs] = _rms(y_scr[:, gs] * _silu_of_twice(z_wide[:, h0:h0 + GROUP_COLS]),
                                        gssd_ref[:, gs]).astype(BF16)

    ssd_tile(pl.program_id(1))


def _ssd(x, mod, conv_hist, h0, p, *, lt, lv, nseq):
    bsz, lpad, d = x.shape
    nt = lpad // lt
    d_inner = p["dskip"].shape[1]
    conv_dim = p["convw"].shape[1]
    n_groups = (conv_dim - d_inner) // (2 * SSD_STATE)
    n_chunks = lt // SSD_CHUNK
    hp = h0.shape[1]
    mod_rows = mod.shape[2]
    conv_rows = (CONV_WIDTH - 1) * nseq
    assert lv >= CONV_WIDTH - 1 and lt % SSD_CHUNK == 0 and d_inner == n_groups * GROUP_COLS
    assert nseq == 1 or (nseq == SUBLANES and lt == lv == lpad == SSD_CHUNK)

    weights = [p["gpre"], p["win_ssd"], p["convw"], p["convb"], p["dtb"], p["alog"],
               p["dskip"], p["gssd"], p["e2"]]
    tile_map = lambda b, t: (b, t, 0)
    seq_map = lambda b, t: (b, 0, 0)
    shared_map = lambda b, t: (0, 0, 0)
    in_specs = [
        pl.BlockSpec((1, lt, d), tile_map),
        pl.BlockSpec((1, 6, mod_rows, d), lambda b, t: (b, 0, 0, 0)),
        pl.BlockSpec((1, CONV_CARRY, conv_dim), seq_map if conv_hist.shape[0] == bsz else shared_map),
        pl.BlockSpec((nseq, hp, SSD_STATE), seq_map if h0.shape[0] == bsz * nseq else shared_map),
    ] + [pl.BlockSpec(memory_space=pltpu.VMEM) for _ in weights]
    out_shape = (
        jax.ShapeDtypeStruct((bsz, lpad, d), BF16),
        jax.ShapeDtypeStruct((bsz, lpad, d_inner), BF16),
        jax.ShapeDtypeStruct((bsz * nseq, hp, SSD_STATE), F32),
        jax.ShapeDtypeStruct((bsz, conv_rows, conv_dim), F32),
    )
    out_specs = (
        pl.BlockSpec((1, lt, d), tile_map),
        pl.BlockSpec((1, lt, d_inner), tile_map),
        pl.BlockSpec((nseq, hp, SSD_STATE), seq_map),
        pl.BlockSpec((1, conv_rows, conv_dim), seq_map),
    )
    scratch = [
        pltpu.VMEM((d // LANES, lt, LANES), F32),
        pltpu.VMEM((CONV_CARRY, conv_dim), F32),
        pltpu.VMEM((lt, d_inner), F32),
        pltpu.VMEM((n_chunks * n_groups, SSD_STATE, SSD_CHUNK), BF16),
        pltpu.VMEM((lt, n_groups * SSD_STATE), BF16),
        pltpu.VMEM((lt, LANES), F32),
        pltpu.VMEM((n_chunks, LANES, SSD_CHUNK), F32),
        pltpu.VMEM((lt, d_inner), F32),
        pltpu.VMEM((lt, d_inner), F32),
        pltpu.VMEM((lt, d_inner), F32),
        pltpu.VMEM((nseq, SSD_STATE, hp), F32),
        pltpu.VMEM((n_groups, SSD_CHUNK, SSD_CHUNK), F32),
        pltpu.VMEM((n_groups, HEADS_PER_GROUP * SSD_CHUNK, GROUP_COLS), BF16),
        pltpu.VMEM((n_groups, SSD_CHUNK, HEADS_PER_GROUP * SSD_CHUNK), BF16),
    ]
    kern = functools.partial(_ssd_kernel, lt=lt, lv=lv, d_inner=d_inner, n_groups=n_groups, nseq=nseq)
    return pl.pallas_call(
        kern, out_shape=out_shape, grid=(bsz, nt), in_specs=in_specs, out_specs=out_specs,
        scratch_shapes=scratch,
        compiler_params=pltpu.CompilerParams(dimension_semantics=("arbitrary", "arbitrary"),
                                             vmem_limit_bytes=VMEM_LIMIT),
        name="ssd",
    )(x, mod, conv_hist, h0, *weights)


def _merge_kernel(x_ref, u_ref, yn_ref, mod_ref, poolh_ref,
                  gpost_ref, win_ref, wssd_ref, wpool_ref, pscale_ref, wo_ref,
                  out_ref, poolo_ref,
                  perm_scr, poolc_scr, ypool_scr,
                  *, lt, lv, pos0, nseq):
    q = SSD_CHUNK
    n_chunks = lt // q
    t = pl.program_id(1)
    packed = nseq > 1
    d_model = x_ref.shape[-1]
    n_slabs = d_model // LANES

    @pl.when(t == 0)
    def _init():
        poolc_scr[...] = poolh_ref[0]

    d_pool = ypool_scr.shape[1]
    p_all = _dot_wt(u_ref[0], win_ref[0:d_pool, :])
    for gi, w in enumerate(POOL_WINDOWS):
        gs = slice(gi * GROUP_COLS, (gi + 1) * GROUP_COLS)
        res = p_all[:, gs]
        prev_tail = poolc_scr[:, gs]
        for c in range(n_chunks):
            rows = slice(c * q, (c + 1) * q)
            cur = res[rows]
            back = _shifted_groups(prev_tail[POOL_CARRY - (w - 1) * SUBLANES:], cur, w - 1, packed)
            win = jnp.concatenate(back + [cur], axis=0)
            span = 1
            while span < w:
                win = win[span * SUBLANES:] + win[:win.shape[0] - span * SUBLANES]
                span *= 2
            if pos0 >= POOL_HIST or c > 0:
                pm = win * (1.0 / w) - cur
            else:
                cnt = jnp.minimum(_tokens((q, GROUP_COLS), 0) + (pos0 + 1), w).astype(F32)
                inv = jnp.where(t == 0, 1.0 / cnt, 1.0 / w)
                pm = win * inv - cur
            ypool_scr[rows, gs] = _dot(pm.astype(BF16), wpool_ref[gi]) * pscale_ref[:, gs]
            prev_tail = cur[SUBLANES:]
        poolc_scr[:, gs] = prev_tail
        if packed:
            poolo_ref[0, :, gs] = res[q - POOL_CARRY:]
        else:
            for j in range(POOL_HIST):
                r = _tile_row(lv - POOL_HIST + j)
                poolo_ref[0, j:j + 1, gs] = res[r:r + 1]

    y_ssd = _dot(yn_ref[0], wssd_ref[...])
    gate_ssd = _sigmoid(_dot_wt(u_ref[0], win_ref[d_pool:d_pool + d_model, :]))
    gate_pool = _sigmoid(_dot_wt(u_ref[0], win_ref[d_pool + d_model:d_pool + 2 * d_model, :]))
    merged = (gate_ssd * y_ssd + gate_pool * ypool_scr[...]).astype(BF16)
    mix = _dot(merged, wo_ref[...])
    branch = mod_ref[0, 2] * _rms(mix, gpost_ref[...])
    for j in range(n_slabs):
        perm_scr[j] = branch[:, j * LANES:(j + 1) * LANES]
    for c in range(n_chunks):
        for i in range(ROW_GROUPS):
            start = c * q + (i // 2 if i % 2 == 0 else q // 2 + i // 2)
            rows = slice(c * q + i * SUBLANES, c * q + (i + 1) * SUBLANES)
            for j in range(n_slabs):
                ls = slice(j * LANES, (j + 1) * LANES)
                out_ref[0, rows, ls] = x_ref[0, rows, ls] + perm_scr[j, pl.ds(start, SUBLANES, stride=SUBLANES), :]


def _merge(x, u, yn, mod, pool_hist, p, *, lt, lv, pos0, nseq):
    bsz, lpad, d = x.shape
    nt = lpad // lt
    d_inner = yn.shape[-1]
    d_pool = p["pscale"].shape[1]
    mod_rows = mod.shape[2]
    pool_rows = POOL_HIST * nseq
    assert lv >= POOL_HIST and d_pool == len(POOL_WINDOWS) * GROUP_COLS
    assert nseq == 1 or (nseq == SUBLANES and lt == lv == lpad == SSD_CHUNK and pos0 >= POOL_HIST)
    weights = [p["gpost_mix"], p["win_merge"], p["wssd"], p["wpool"], p["pscale"], p["wo"]]
    in_specs = [
        pl.BlockSpec((1, lt, d), lambda b, t: (b, t, 0)),
        pl.BlockSpec((1, lt, d), lambda b, t: (b, t, 0)),
        pl.BlockSpec((1, lt, d_inner), lambda b, t: (b, t, 0)),
        pl.BlockSpec((1, 6, mod_rows, d), lambda b, t: (b, 0, 0, 0)),
        pl.BlockSpec((1, POOL_CARRY, d_pool), lambda b, t: (b if pool_hist.shape[0] == bsz else 0, 0, 0)),
    ] + [pl.BlockSpec(memory_space=pltpu.VMEM) for _ in weights]
    out_shape = (
        jax.ShapeDtypeStruct((bsz, lpad, d), F32),
        jax.ShapeDtypeStruct((bsz, pool_rows, d_pool), F32),
    )
    out_specs = (
        pl.BlockSpec((1, lt, d), lambda b, t: (b, t, 0)),
        pl.BlockSpec((1, pool_rows, d_pool), lambda b, t: (b, 0, 0)),
    )
    scratch = [
        pltpu.VMEM((d // LANES, lt, LANES), F32),
        pltpu.VMEM((POOL_CARRY, d_pool), F32),
        pltpu.VMEM((lt, d_pool), F32),
    ]
    kern = functools.partial(_merge_kernel, lt=lt, lv=lv, pos0=pos0, nseq=nseq)
    return pl.pallas_call(
        kern, out_shape=out_shape, grid=(bsz, nt), in_specs=in_specs, out_specs=out_specs,
        scratch_shapes=scratch,
        compiler_params=pltpu.CompilerParams(dimension_semantics=("arbitrary", "arbitrary"),
                                             vmem_limit_bytes=VMEM_LIMIT),
        name="merge",
    )(x, u, yn, mod, pool_hist, *weights)


def _mlp_kernel(x_ref, mod_ref, gpre_ref, gpost_ref, wup_ref, wdown_ref, out_ref, v_scr):
    x = x_ref[0]
    v_scr[...] = (_rms(x, gpre_ref[...] * (1.0 + mod_ref[0, 4])) + mod_ref[0, 3]).astype(BF16)
    acc = None
    for c0 in range(0, wup_ref.shape[1], PROJ_COLS):
        h = jnp.square(jnp.maximum(_dot(v_scr[...], wup_ref[:, c0:c0 + PROJ_COLS]), 0.0)).astype(BF16)
        part = _dot(h, wdown_ref[c0:c0 + PROJ_COLS, :])
        acc = part if acc is None else acc + part
    out_ref[0] = x + mod_ref[0, 5] * _rms(acc, gpost_ref[...])


def _mlp(x, mod, p, *, tm):
    bsz, lpad, d = x.shape
    nt = lpad // tm
    weights = [p["gpre_mlp"], p["gpost_mlp"], p["wup"], p["wdown"]]
    return pl.pallas_call(
        _mlp_kernel,
        out_shape=jax.ShapeDtypeStruct((bsz, lpad, d), F32),
        grid=(bsz, nt),
        in_specs=[pl.BlockSpec((1, tm, d), lambda b, t: (b, t, 0)),
                  pl.BlockSpec((1, 6, mod.shape[2], d), lambda b, t: (b, 0, 0, 0))]
                 + [pl.BlockSpec(memory_space=pltpu.VMEM) for _ in weights],
        out_specs=pl.BlockSpec((1, tm, d), lambda b, t: (b, t, 0)),
        scratch_shapes=[pltpu.VMEM((tm, d), BF16)],
        compiler_params=pltpu.CompilerParams(dimension_semantics=("arbitrary", "arbitrary"),
                                             vmem_limit_bytes=VMEM_LIMIT),
        name="mlp",
    )(x, mod, *weights)


def _prep_weights(g_pre_mix, g_post_mix, g_pre_mlp, g_post_mlp, w_in, conv_w, conv_b, dt_bias, a_log,
                  d_skip, g_ssd_norm, w_ssd_out, w_pool_group, pool_scale, w_o, w_up, w_down):
    d_inner = w_ssd_out.shape[0]
    conv_dim = conv_w.shape[1]
    n_heads = dt_bias.shape[0]
    d_pool = pool_scale.shape[0]
    s_z, s_xbc = d_inner, d_inner + conv_dim
    s_dt, s_pool = s_xbc + n_heads, s_xbc + n_heads + d_pool
    pad = LANES - n_heads
    head_of_col = jnp.arange(d_inner) // SSD_HEAD_DIM
    e1 = (jnp.arange(LANES)[:, None] == head_of_col[None, :]).astype(BF16)
    row = lambda a: a.reshape(1, -1).astype(F32)
    z_rows = (jnp.arange(w_in.shape[1]) < s_z)[:, None]
    w_bf = (jnp.swapaxes(w_in, 0, 1) * jnp.where(z_rows, 0.5, 1.0)).astype(BF16)
    return dict(
        gpre=row(g_pre_mix), gpost_mix=row(g_post_mix), gpre_mlp=row(g_pre_mlp), gpost_mlp=row(g_post_mlp),
        win_ssd=w_bf[:s_xbc + LANES], win_merge=w_bf[s_dt:],
        convw=0.5 * conv_w.astype(F32), convb=0.5 * row(conv_b),
        dtb=jnp.pad(row(dt_bias), ((0, 0), (0, pad))), alog=jnp.pad(row(a_log), ((0, 0), (0, pad))),
        dskip=jnp.repeat(row(d_skip), SSD_HEAD_DIM, axis=1), gssd=row(g_ssd_norm),
        e2=jnp.concatenate([e1, e1], axis=0),
        wssd=w_ssd_out.astype(BF16), wpool=w_pool_group.astype(BF16), pscale=row(pool_scale),
        wo=w_o.astype(BF16), wup=w_up.astype(BF16), wdown=w_down.astype(BF16),
    )


def _history_rows(hist):
    bsz, n, c = hist.shape
    out = jnp.zeros((bsz, n, SUBLANES, c), hist.dtype).at[:, :, SUBLANES - 1, :].set(hist)
    return out.reshape(bsz, n * SUBLANES, c)


def _layer(x, mod, conv_hist, pool_hist, h0, p, *, pos0):
    bsz, seq, d = x.shape
    heads, hdim, nstate = h0.shape[1:]
    h0 = h0.reshape(h0.shape[0], heads * hdim, nstate)
    if bsz == SUBLANES and seq == ROW_GROUPS and pos0 >= POOL_HIST:
        return _layer_packed(x, mod, conv_hist, pool_hist, h0, p, pos0=pos0, state_shape=(heads, hdim, nstate))
    if seq % MIX_TILE == 0:
        lt, lpad = MIX_TILE, seq
    else:
        lt = lpad = -(-seq // SSD_CHUNK) * SSD_CHUNK
    lv = seq - (lpad - lt)
    xp = x if lpad == seq else jnp.pad(x, ((0, 0), (0, lpad - seq), (0, 0)))
    mod = mod[:, :, None, :]
    u, yn, h_new, conv_state = _ssd(xp, mod, _history_rows(conv_hist), h0, p, lt=lt, lv=lv, nseq=1)
    lm = MERGE_TILE if lpad % MERGE_TILE == 0 else lt
    x1, pool_state = _merge(xp, u, yn, mod, _history_rows(pool_hist), p, lt=lm, lv=lv + lm - lt, pos0=pos0, nseq=1)
    tm = MLP_TILE if lpad % MLP_TILE == 0 else lpad
    y = _mlp(x1, mod, p, tm=tm)
    if lpad != seq:
        y = y[:, :seq]
    return y, h_new.reshape(bsz, heads, hdim, nstate), conv_state, pool_state


def _layer_packed(x, mod, conv_hist, pool_hist, h0, p, *, pos0, state_shape):
    bsz, seq, d = x.shape
    rows = bsz * seq
    by_step = lambda a: a.transpose(1, 0, 2).reshape(1, a.shape[1] * bsz, a.shape[2])
    from_step = lambda a, n: a.reshape(n, bsz, a.shape[-1]).transpose(1, 0, 2)
    mod_t = mod.transpose(1, 0, 2)
    mod_tok = jnp.repeat(mod_t, seq, axis=1)[None]
    mod_perm = jnp.tile(mod_t, (1, seq, 1))[None]
    xp = x.reshape(1, rows, d)
    u, yn, h_new, conv_state = _ssd(xp, mod_tok, by_step(conv_hist), h0, p, lt=rows, lv=rows, nseq=bsz)
    x1, pool_state = _merge(xp, u, yn, mod_perm, by_step(pool_hist), p, lt=rows, lv=rows, pos0=pos0, nseq=bsz)
    y = _mlp(x1, mod_tok, p, tm=rows)
    return (y.reshape(bsz, seq, d), h_new.reshape((bsz,) + state_shape),
            from_step(conv_state, CONV_WIDTH - 1), from_step(pool_state, POOL_HIST))


def kernel(x_prompt, x_sample, state_ssm, state_conv, state_pool, c_prompt, c_sample, w_ada, b_ada, g_pre_mix, g_post_mix, g_pre_mlp, g_post_mlp, w_in, conv_w, conv_b, dt_bias, a_log, d_skip, g_ssd_norm, w_ssd_out, w_pool_group, pool_scale, w_o, w_up, w_down):
    yp, ys = x_prompt, x_sample
    bp, d = x_prompt.shape[0], x_prompt.shape[2]
    bs = x_sample.shape[0]
    depth = w_in.shape[0]
    outs = [[] for _ in range(6)]
    for l in range(depth):
        p = _prep_weights(g_pre_mix[l], g_post_mix[l], g_pre_mlp[l], g_post_mlp[l], w_in[l], conv_w[l],
                          conv_b[l], dt_bias[l], a_log[l], d_skip[l], g_ssd_norm[l], w_ssd_out[l],
                          w_pool_group[l], pool_scale[l], w_o[l], w_up[l], w_down[l])
        mod = _adaln(jnp.concatenate([c_prompt, c_sample], axis=0), w_ada[l], b_ada[l])
        mod_p = mod[:bp].reshape(bp, 6, d)
        mod_s = mod[bp:].reshape(bs, 6, d)
        zc = jnp.zeros((1,) + state_conv.shape[2:], yp.dtype)
        zp = jnp.zeros((1,) + state_pool.shape[2:], yp.dtype)
        zh = jnp.zeros((1,) + state_ssm.shape[2:], yp.dtype)
        yp, h_p, c_p, p_p = _layer(yp, mod_p, zc, zp, zh, p, pos0=0)
        ys, h_s, c_s, p_s = _layer(ys, mod_s, state_conv[l], state_pool[l], state_ssm[l], p, pos0=PAST_LEN)
        for lst, val in zip(outs, (h_p, c_p, p_p, h_s, c_s, p_s)):
            lst.append(val)
    return (yp, ys) + tuple(jnp.stack(o) for o in outs)
```

```python
import functools

import jax
import jax.numpy as jnp
from jax import lax
from jax.experimental import pallas as pl
from jax.experimental.pallas import tpu as pltpu

F32 = jnp.float32
BF16 = jnp.bfloat16

EPS = 1e-6
LOG2E = 1.4426950408889634
PAST_LEN = 2048
POOL_WINDOWS = (2, 4, 8, 16)
POOL_HIST = max(POOL_WINDOWS) - 1
CONV_WIDTH = 4
SSD_HEAD_DIM = 64
SSD_STATE = 128
LANES = 128
SUBLANES = 8
SSD_CHUNK = 128
ROW_GROUPS = SSD_CHUNK // SUBLANES
GROUP_COLS = 256
HEADS_PER_GROUP = GROUP_COLS // SSD_HEAD_DIM
PROJ_COLS = 2048
CONV_COLS = SSD_STATE
CONV_CARRY = (CONV_WIDTH - 1) * SUBLANES
POOL_CARRY = POOL_HIST * SUBLANES
MIX_TILE = 512
MERGE_TILE = 1024
MLP_TILE = 1024
VMEM_LIMIT = 60 * 1024 * 1024


def _row_of_token(tok):
    return (tok % ROW_GROUPS) * SUBLANES + tok // ROW_GROUPS


def _tile_row(tok):
    return (tok // SSD_CHUNK) * SSD_CHUNK + _row_of_token(tok % SSD_CHUNK)


def _tokens(shape, axis):
    r = lax.broadcasted_iota(jnp.int32, shape, axis)
    return (r // SSD_CHUNK) * SSD_CHUNK + (r % SUBLANES) * ROW_GROUPS + (r % SSD_CHUNK) // SUBLANES


def _dot(a, b):
    return jnp.dot(a, b, preferred_element_type=F32)


def _dot_wt(a, w_t):
    return lax.dot_general(a, w_t, (((1,), (1,)), ((), ())), preferred_element_type=F32)


def _sigmoid(v):
    return 1.0 / (1.0 + jnp.exp2(v * (-LOG2E)))


def _silu(v):
    return v * _sigmoid(v)


def _silu_of_twice(h):
    return h + h * jnp.tanh(h)


def _rms(v, g):
    return v * lax.rsqrt(jnp.mean(v * v, axis=-1, keepdims=True) + EPS) * g


def _shifted_groups(prev_tail, cur, n_back, packed):
    if packed:
        return [prev_tail[j * SUBLANES:(j + 1) * SUBLANES] for j in range(n_back)]
    first = lax.broadcasted_iota(jnp.int32, (SUBLANES, cur.shape[1]), 0) == 0
    tail = cur[SSD_CHUNK - n_back * SUBLANES:]
    parts = []
    for j in range(n_back):
        rs = slice(j * SUBLANES, (j + 1) * SUBLANES)
        parts.append(jnp.where(first, pltpu.roll(prev_tail[rs], 1, axis=0), pltpu.roll(tail[rs], 1, axis=0)))
    return parts


def _adaln_kernel(c_ref, w_ref, b_ref, o_ref):
    c = c_ref[...]
    o_ref[...] = _dot(_silu(c).astype(BF16), w_ref[...].astype(BF16)) + b_ref[...]


def _adaln(c, w_ada, b_ada):
    rows, d = c.shape
    n = w_ada.shape[1]
    bn = d
    return pl.pallas_call(
        _adaln_kernel,
        out_shape=jax.ShapeDtypeStruct((rows, n), F32),
        grid=(n // bn,),
        in_specs=[pl.BlockSpec((rows, d), lambda j: (0, 0)),
                  pl.BlockSpec((d, bn), lambda j: (0, j)),
                  pl.BlockSpec((1, bn), lambda j: (0, j))],
        out_specs=pl.BlockSpec((rows, bn), lambda j: (0, j)),
        compiler_params=pltpu.CompilerParams(dimension_semantics=("arbitrary",)),
        name="adaln",
    )(c, w_ada, b_ada.reshape(1, n))


def _ssd_kernel(x_ref, mod_ref, convh_ref, h0_ref,
                gpre_ref, win_ref, convw_ref, convb_ref, dtb_ref, alog_ref, dskip_ref,
                gssd_ref, e2_ref,
                u_ref, yn_ref, hout_ref, convo_ref,
                perm_scr, convc_scr, xr_scr, bmt_scr, cm_scr, acs_scr, acst_scr,
                ea_scr, dte_scr, y_scr, st_scr, cb_scr, rcat_scr, mcat_scr,
                *, lt, lv, d_inner, n_groups, nseq):
    q = SSD_CHUNK
    n_chunks = lt // q
    d_bc = n_groups * SSD_STATE
    d_model = x_ref.shape[-1]
    n_slabs = d_model // LANES
    packed = nseq > 1

    def prenorm(dst_ref):
        u = _rms(x_ref[0], gpre_ref[...] * (1.0 + mod_ref[0, 1])) + mod_ref[0, 0]
        for c in range(n_chunks):
            for i in range(ROW_GROUPS):
                start = c * q + (i // 2 if i % 2 == 0 else q // 2 + i // 2)
                r0 = c * q + i * SUBLANES
                for j in range(n_slabs):
                    perm_scr[j, pl.ds(start, SUBLANES, stride=SUBLANES), :] = (
                        u[r0:r0 + SUBLANES, j * LANES:(j + 1) * LANES])
        for j in range(n_slabs):
            dst_ref[:, j * LANES:(j + 1) * LANES] = perm_scr[j].astype(BF16)

    def ssd_tile(t):
        last_t = pl.num_programs(1) - 1

        @pl.when(t == 0)
        def _init():
            convc_scr[...] = convh_ref[0]
            rcat_scr[...] = jnp.zeros(rcat_scr.shape, BF16)
            for s in range(nseq):
                for g in range(n_groups):
                    cs = slice(g * GROUP_COLS, (g + 1) * GROUP_COLS)
                    st_scr[s, :, cs] = h0_ref[s, cs, :].T

        prenorm(u_ref.at[0])

        lane = lax.broadcasted_iota(jnp.int32, (1, LANES), 1)
        n_heads = d_inner // SSD_HEAD_DIM
        r_dt = d_inner + d_inner + 2 * d_bc
        dt = jax.nn.softplus(_dot_wt(u_ref[0], win_ref[r_dt:r_dt + LANES, :]) + dtb_ref[...])
        if lv < lt:
            dt = jnp.where(_tokens((lt, LANES), 0) < lv, dt, 0.0)
        a_row = jnp.where(lane < n_heads, -jnp.exp(alog_ref[...]), 0.0)
        da = dt * a_row
        tok_l, tok_s = _tokens((q, q), 0), _tokens((q, q), 1)
        causal = tok_l >= tok_s
        if packed:
            causal = causal & (tok_l // ROW_GROUPS == tok_s // ROW_GROUPS)
        tri3 = jnp.concatenate([jnp.where(causal, 1.0, 0.0).astype(BF16)] * 3, axis=1)
        ea_parts, dte_parts = [], []
        for c in range(n_chunks):
            da_c = da[c * q:(c + 1) * q]
            hi = da_c.astype(BF16)
            rest = da_c - hi.astype(F32)
            mid = rest.astype(BF16)
            lo = (rest - mid.astype(F32)).astype(BF16)
            acs = _dot(tri3, jnp.concatenate([hi, mid, lo], axis=0))
            acs_scr[c * q:(c + 1) * q, :] = acs * LOG2E
            acst_scr[c] = (acs * LOG2E).T
            ea_parts.append(jnp.exp(acs))
            last = jnp.concatenate([acs[q - SUBLANES:]] * ROW_GROUPS, axis=0) if packed else acs[q - 1:q, :]
            dte_parts.append(jnp.exp(last - acs))

        def split_hi_lo(v):
            hi = v.astype(BF16)
            lo = (v - hi.astype(F32)).astype(BF16)
            return jnp.concatenate([hi, lo], axis=1)

        dt2 = split_hi_lo(dt)
        ea2 = split_hi_lo(jnp.concatenate(ea_parts, axis=0))
        dte2 = split_hi_lo(jnp.concatenate(dte_parts, axis=0))
        for c0 in range(0, d_inner, PROJ_COLS):
            cs = slice(c0, c0 + PROJ_COLS)
            ea_scr[:, cs] = _dot(ea2, e2_ref[:, cs])
            dte_scr[:, cs] = _dot(dte2, e2_ref[:, cs])

        for p0 in range(0, d_inner + 2 * d_bc, PROJ_COLS):
            ps = slice(p0, p0 + PROJ_COLS)
            res_wide = _dot_wt(u_ref[0], win_ref[d_inner + p0:d_inner + p0 + PROJ_COLS, :])
            dt_wide = _dot(dt2, e2_ref[:, ps]) if p0 < d_inner else None
            for h0 in range(0, PROJ_COLS, CONV_COLS):
                c0 = p0 + h0
                cs = slice(c0, c0 + CONV_COLS)
                res = res_wide[:, h0:h0 + CONV_COLS]
                prev_tail = convc_scr[:, cs]
                for c in range(n_chunks):
                    rows = slice(c * q, (c + 1) * q)
                    cur = res[rows]
                    ext = jnp.concatenate(_shifted_groups(prev_tail, cur, CONV_WIDTH - 1, packed) + [cur], axis=0)
                    acc = convb_ref[:, cs] + ext[CONV_CARRY:] * convw_ref[CONV_WIDTH - 1:CONV_WIDTH, cs]
                    for k in range(1, CONV_WIDTH):
                        lo = CONV_CARRY - k * SUBLANES
                        acc = acc + ext[lo:lo + q] * convw_ref[CONV_WIDTH - 1 - k:CONV_WIDTH - k, cs]
                    v = _silu_of_twice(acc)
                    if c0 < d_inner:
                        y_scr[rows, cs] = v * dskip_ref[:, cs]
                        xr_scr[rows, cs] = v * dt_wide[rows, h0:h0 + CONV_COLS]
                    elif c0 < d_inner + d_bc:
                        b0 = c0 - d_inner
                        bmt_scr[c * n_groups + b0 // SSD_STATE] = v.T.astype(BF16)
                    else:
                        b0 = c0 - d_inner - d_bc
                        cm_scr[rows, b0:b0 + CONV_COLS] = v.astype(BF16)
                    prev_tail = cur[q - CONV_CARRY:]
                convc_scr[:, cs] = prev_tail
                if packed:
                    convo_ref[0, :, cs] = res[q - CONV_CARRY:]
                else:
                    for j in range(CONV_WIDTH - 1):
                        r = _tile_row(lv - (CONV_WIDTH - 1) + j)
                        convo_ref[0, j:j + 1, cs] = res[r:r + 1]

        if packed:
            row_seq = lax.broadcasted_iota(jnp.int32, (q, SSD_STATE), 0) % SUBLANES
            col_seq = lax.broadcasted_iota(jnp.int32, (SSD_STATE, q), 1) % SUBLANES
            row_masks = [jnp.where(row_seq == s, 1.0, 0.0).astype(BF16) for s in range(nseq)]
            col_masks = [jnp.where(col_seq == s, 1.0, 0.0).astype(BF16) for s in range(nseq)]

        def chunk_body(c, carry):
            r0 = pl.multiple_of(c * q, q)
            crow = pl.ds(r0, q)
            for g in range(n_groups):
                gs = slice(g * GROUP_COLS, (g + 1) * GROUP_COLS)
                ns = slice(g * SSD_STATE, (g + 1) * SSD_STATE)
                cb_scr[g] = _dot(cm_scr[crow, ns], bmt_scr[c * n_groups + g])
                xr_bf = xr_scr[crow, gs].astype(BF16)
                for j in range(HEADS_PER_GROUP):
                    ls = slice(j * SSD_HEAD_DIM, (j + 1) * SSD_HEAD_DIM)
                    rcat_scr[g, j * q:(j + 1) * q, ls] = xr_bf[:, ls]
            acs = acs_scr[crow, :]
            acst = acst_scr[c]
            for g in range(n_groups):
                cb = cb_scr[g]
                for j in range(HEADS_PER_GROUP):
                    h = g * HEADS_PER_GROUP + j
                    seg = acs[:, h:h + 1] - acst[h:h + 1, :]
                    dec = jnp.exp2(jnp.where(causal, seg, -jnp.inf))
                    mcat_scr[g, :, j * q:(j + 1) * q] = (cb * dec).astype(BF16)
            for g in range(n_groups):
                gs = slice(g * GROUP_COLS, (g + 1) * GROUP_COLS)
                ns = slice(g * SSD_STATE, (g + 1) * SSD_STATE)
                y_diag = _dot(mcat_scr[g], rcat_scr[g])
                xrd = (xr_scr[crow, gs] * dte_scr[crow, gs]).astype(BF16)
                if packed:
                    cg = cm_scr[crow, ns]
                    c_cat = jnp.concatenate([cg * row_masks[s] for s in range(nseq)], axis=1)
                    st_cat = jnp.concatenate([st_scr[s, :, gs].astype(BF16) for s in range(nseq)], axis=0)
                    y_off = _dot(c_cat, st_cat) * ea_scr[crow, gs]
                    bt = bmt_scr[c * n_groups + g]
                    for s in range(nseq):
                        cd = ea_scr[q - SUBLANES + s:q - SUBLANES + s + 1, gs]
                        st_scr[s, :, gs] = st_scr[s, :, gs] * cd + _dot(bt * col_masks[s], xrd)
                else:
                    st = st_scr[0, :, gs]
                    y_off = _dot(cm_scr[crow, ns], st.astype(BF16)) * ea_scr[crow, gs]
                    cd = ea_scr[pl.ds(r0 + q - 1, 1), gs]
                    st_scr[0, :, gs] = st * cd + _dot(bmt_scr[c * n_groups + g], xrd)
                y_scr[crow, gs] = y_scr[crow, gs] + y_diag + y_off
            return carry

        lax.fori_loop(0, n_chunks, chunk_body, 0)

        @pl.when(t == last_t)
        def _ssm_state():
            for s in range(nseq):
                for g in range(n_groups):
                    cs = slice(g * GROUP_COLS, (g + 1) * GROUP_COLS)
                    hout_ref[s, cs, :] = st_scr[s, :, cs].T

        for p0 in range(0, d_inner, PROJ_COLS):
            z_wide = _dot_wt(u_ref[0], win_ref[p0:p0 + PROJ_COLS, :])
            for h0 in range(0, PROJ_COLS, GROUP_COLS):
                gs = slice(p0 + h0, p0 + h0 + GROUP_COLS)
                yn_ref[0, :, gs] = _rms(y_scr[:, gs] * _silu_of_twice(z_wide[:, h0:h0 + GROUP_COLS]),
                                        gssd_ref[:, gs]).astype(BF16)

    ssd_tile(pl.program_id(1))


def _ssd(x, mod, conv_hist, h0, p, *, lt, lv, nseq):
    bsz, lpad, d = x.shape
    nt = lpad // lt
    d_inner = p["dskip"].shape[1]
    conv_dim = p["convw"].shape[1]
    n_groups = (conv_dim - d_inner) // (2 * SSD_STATE)
    n_chunks = lt // SSD_CHUNK
    hp = h0.shape[1]
    mod_rows = mod.shape[2]
    conv_rows = (CONV_WIDTH - 1) * nseq
    assert lv >= CONV_WIDTH - 1 and lt % SSD_CHUNK == 0 and d_inner == n_groups * GROUP_COLS
    assert nseq == 1 or (nseq == SUBLANES and lt == lv == lpad == SSD_CHUNK)

    weights = [p["gpre"], p["win_ssd"], p["convw"], p["convb"], p["dtb"], p["alog"],
               p["dskip"], p["gssd"], p["e2"]]
    tile_map = lambda b, t: (b, t, 0)
    seq_map = lambda b, t: (b, 0, 0)
    shared_map = lambda b, t: (0, 0, 0)
    in_specs = [
        pl.BlockSpec((1, lt, d), tile_map),
        pl.BlockSpec((1, 6, mod_rows, d), lambda b, t: (b, 0, 0, 0)),
        pl.BlockSpec((1, CONV_CARRY, conv_dim), seq_map if conv_hist.shape[0] == bsz else shared_map),
        pl.BlockSpec((nseq, hp, SSD_STATE), seq_map if h0.shape[0] == bsz * nseq else shared_map),
    ] + [pl.BlockSpec(memory_space=pltpu.VMEM) for _ in weights]
    out_shape = (
        jax.ShapeDtypeStruct((bsz, lpad, d), BF16),
        jax.ShapeDtypeStruct((bsz, lpad, d_inner), BF16),
        jax.ShapeDtypeStruct((bsz * nseq, hp, SSD_STATE), F32),
        jax.ShapeDtypeStruct((bsz, conv_rows, conv_dim), F32),
    )
    out_specs = (
        pl.BlockSpec((1, lt, d), tile_map),
        pl.BlockSpec((1, lt, d_inner), tile_map),
        pl.BlockSpec((nseq, hp, SSD_STATE), seq_map),
        pl.BlockSpec((1, conv_rows, conv_dim), seq_map),
    )
    scratch = [
        pltpu.VMEM((d // LANES, lt, LANES), F32),
        pltpu.VMEM((CONV_CARRY, conv_dim), F32),
        pltpu.VMEM((lt, d_inner), F32),
        pltpu.VMEM((n_chunks * n_groups, SSD_STATE, SSD_CHUNK), BF16),
        pltpu.VMEM((lt, n_groups * SSD_STATE), BF16),
        pltpu.VMEM((lt, LANES), F32),
        pltpu.VMEM((n_chunks, LANES, SSD_CHUNK), F32),
        pltpu.VMEM((lt, d_inner), F32),
        pltpu.VMEM((lt, d_inner), F32),
        pltpu.VMEM((lt, d_inner), F32),
        pltpu.VMEM((nseq, SSD_STATE, hp), F32),
        pltpu.VMEM((n_groups, SSD_CHUNK, SSD_CHUNK), F32),
        pltpu.VMEM((n_groups, HEADS_PER_GROUP * SSD_CHUNK, GROUP_COLS), BF16),
        pltpu.VMEM((n_groups, SSD_CHUNK, HEADS_PER_GROUP * SSD_CHUNK), BF16),
    ]
    kern = functools.partial(_ssd_kernel, lt=lt, lv=lv, d_inner=d_inner, n_groups=n_groups, nseq=nseq)
    return pl.pallas_call(
        kern, out_shape=out_shape, grid=(bsz, nt), in_specs=in_specs, out_specs=out_specs,
        scratch_shapes=scratch,
        compiler_params=pltpu.CompilerParams(dimension_semantics=("arbitrary", "arbitrary"),
                                             vmem_limit_bytes=VMEM_LIMIT),
        name="ssd",
    )(x, mod, conv_hist, h0, *weights)


def _merge_kernel(x_ref, u_ref, yn_ref, mod_ref, poolh_ref,
                  gpost_ref, win_ref, wssd_ref, wpool_ref, pscale_ref, wo_ref,
                  out_ref, poolo_ref,
                  perm_scr, poolc_scr, ypool_scr,
                  *, lt, lv, pos0, nseq):
    q = SSD_CHUNK
    n_chunks = lt // q
    t = pl.program_id(1)
    packed = nseq > 1
    d_model = x_ref.shape[-1]
    n_slabs = d_model // LANES

    @pl.when(t == 0)
    def _init():
        poolc_scr[...] = poolh_ref[0]

    d_pool = ypool_scr.shape[1]
    p_all = _dot_wt(u_ref[0], win_ref[0:d_pool, :])
    for gi, w in enumerate(POOL_WINDOWS):
        gs = slice(gi * GROUP_COLS, (gi + 1) * GROUP_COLS)
        res = p_all[:, gs]
        prev_tail = poolc_scr[:, gs]
        for c in range(n_chunks):
            rows = slice(c * q, (c + 1) * q)
            cur = res[rows]
            back = _shifted_groups(prev_tail[POOL_CARRY - (w - 1) * SUBLANES:], cur, w - 1, packed)
            win = jnp.concatenate(back + [cur], axis=0)
            span = 1
            while span < w:
                win = win[span * SUBLANES:] + win[:win.shape[0] - span * SUBLANES]
                span *= 2
            if pos0 >= POOL_HIST or c > 0:
                pm = win * (1.0 / w) - cur
            else:
                cnt = jnp.minimum(_tokens((q, GROUP_COLS), 0) + (pos0 + 1), w).astype(F32)
                inv = jnp.where(t == 0, 1.0 / cnt, 1.0 / w)
                pm = win * inv - cur
            ypool_scr[rows, gs] = _dot(pm.astype(BF16), wpool_ref[gi]) * pscale_ref[:, gs]
            prev_tail = cur[SUBLANES:]
        poolc_scr[:, gs] = prev_tail
        if packed:
            poolo_ref[0, :, gs] = res[q - POOL_CARRY:]
        else:
            for j in range(POOL_HIST):
                r = _tile_row(lv - POOL_HIST + j)
                poolo_ref[0, j:j + 1, gs] = res[r:r + 1]

    y_ssd = _dot(yn_ref[0], wssd_ref[...])
    gate_ssd = _sigmoid(_dot_wt(u_ref[0], win_ref[d_pool:d_pool + d_model, :]))
    gate_pool = _sigmoid(_dot_wt(u_ref[0], win_ref[d_pool + d_model:d_pool + 2 * d_model, :]))
    merged = (gate_ssd * y_ssd + gate_pool * ypool_scr[...]).astype(BF16)
    mix = _dot(merged, wo_ref[...])
    branch = mod_ref[0, 2] * _rms(mix, gpost_ref[...])
    for j in range(n_slabs):
        perm_scr[j] = branch[:, j * LANES:(j + 1) * LANES]
    for c in range(n_chunks):
        for i in range(ROW_GROUPS):
            start = c * q + (i // 2 if i % 2 == 0 else q // 2 + i // 2)
            rows = slice(c * q + i * SUBLANES, c * q + (i + 1) * SUBLANES)
            for j in range(n_slabs):
                ls = slice(j * LANES, (j + 1) * LANES)
                out_ref[0, rows, ls] = x_ref[0, rows, ls] + perm_scr[j, pl.ds(start, SUBLANES, stride=SUBLANES), :]


def _merge(x, u, yn, mod, pool_hist, p, *, lt, lv, pos0, nseq):
    bsz, lpad, d = x.shape
    nt = lpad // lt
    d_inner = yn.shape[-1]
    d_pool = p["pscale"].shape[1]
    mod_rows = mod.shape[2]
    pool_rows = POOL_HIST * nseq
    assert lv >= POOL_HIST and d_pool == len(POOL_WINDOWS) * GROUP_COLS
    assert nseq == 1 or (nseq == SUBLANES and lt == lv == lpad == SSD_CHUNK and pos0 >= POOL_HIST)
    weights = [p["gpost_mix"], p["win_merge"], p["wssd"], p["wpool"], p["pscale"], p["wo"]]
    in_specs = [
        pl.BlockSpec((1, lt, d), lambda b, t: (b, t, 0)),
        pl.BlockSpec((1, lt, d), lambda b, t: (b, t, 0)),
        pl.BlockSpec((1, lt, d_inner), lambda b, t: (b, t, 0)),
        pl.BlockSpec((1, 6, mod_rows, d), lambda b, t: (b, 0, 0, 0)),
        pl.BlockSpec((1, POOL_CARRY, d_pool), lambda b, t: (b if pool_hist.shape[0] == bsz else 0, 0, 0)),
    ] + [pl.BlockSpec(memory_space=pltpu.VMEM) for _ in weights]
    out_shape = (
        jax.ShapeDtypeStruct((bsz, lpad, d), F32),
        jax.ShapeDtypeStruct((bsz, pool_rows, d_pool), F32),
    )
    out_specs = (
        pl.BlockSpec((1, lt, d), lambda b, t: (b, t, 0)),
        pl.BlockSpec((1, pool_rows, d_pool), lambda b, t: (b, 0, 0)),
    )
    scratch = [
        pltpu.VMEM((d // LANES, lt, LANES), F32),
        pltpu.VMEM((POOL_CARRY, d_pool), F32),
        pltpu.VMEM((lt, d_pool), F32),
    ]
    kern = functools.partial(_merge_kernel, lt=lt, lv=lv, pos0=pos0, nseq=nseq)
    return pl.pallas_call(
        kern, out_shape=out_shape, grid=(bsz, nt), in_specs=in_specs, out_specs=out_specs,
        scratch_shapes=scratch,
        compiler_params=pltpu.CompilerParams(dimension_semantics=("arbitrary", "arbitrary"),
                                             vmem_limit_bytes=VMEM_LIMIT),
        name="merge",
    )(x, u, yn, mod, pool_hist, *weights)


def _mlp_kernel(x_ref, mod_ref, gpre_ref, gpost_ref, wup_ref, wdown_ref, out_ref, v_scr):
    x = x_ref[0]
    v_scr[...] = (_rms(x, gpre_ref[...] * (1.0 + mod_ref[0, 4])) + mod_ref[0, 3]).astype(BF16)
    acc = None
    for c0 in range(0, wup_ref.shape[1], PROJ_COLS):
        h = jnp.square(jnp.maximum(_dot(v_scr[...], wup_ref[:, c0:c0 + PROJ_COLS]), 0.0)).astype(BF16)
        part = _dot(h, wdown_ref[c0:c0 + PROJ_COLS, :])
        acc = part if acc is None else acc + part
    out_ref[0] = x + mod_ref[0, 5] * _rms(acc, gpost_ref[...])


def _mlp(x, mod, p, *, tm):
    bsz, lpad, d = x.shape
    nt = lpad // tm
    weights = [p["gpre_mlp"], p["gpost_mlp"], p["wup"], p["wdown"]]
    return pl.pallas_call(
        _mlp_kernel,
        out_shape=jax.ShapeDtypeStruct((bsz, lpad, d), F32),
        grid=(bsz, nt),
        in_specs=[pl.BlockSpec((1, tm, d), lambda b, t: (b, t, 0)),
                  pl.BlockSpec((1, 6, mod.shape[2], d), lambda b, t: (b, 0, 0, 0))]
                 + [pl.BlockSpec(memory_space=pltpu.VMEM) for _ in weights],
        out_specs=pl.BlockSpec((1, tm, d), lambda b, t: (b, t, 0)),
        scratch_shapes=[pltpu.VMEM((tm, d), BF16)],
        compiler_params=pltpu.CompilerParams(dimension_semantics=("arbitrary", "arbitrary"),
                                             vmem_limit_bytes=VMEM_LIMIT),
        name="mlp",
    )(x, mod, *weights)


def _prep_weights(g_pre_mix, g_post_mix, g_pre_mlp, g_post_mlp, w_in, conv_w, conv_b, dt_bias, a_log,
                  d_skip, g_ssd_norm, w_ssd_out, w_pool_group, pool_scale, w_o, w_up, w_down):
    d_inner = w_ssd_out.shape[0]
    conv_dim = conv_w.shape[1]
    n_heads = dt_bias.shape[0]
    d_pool = pool_scale.shape[0]
    s_z, s_xbc = d_inner, d_inner + conv_dim
    s_dt, s_pool = s_xbc + n_heads, s_xbc + n_heads + d_pool
    pad = LANES - n_heads
    head_of_col = jnp.arange(d_inner) // SSD_HEAD_DIM
    e1 = (jnp.arange(LANES)[:, None] == head_of_col[None, :]).astype(BF16)
    row = lambda a: a.reshape(1, -1).astype(F32)
    w_t = jnp.swapaxes(w_in, 0, 1)
    r_ssd = s_xbc + LANES
    z_scale = jnp.where(jnp.arange(r_ssd) < s_z, 0.5, 1.0)[:, None]
    return dict(
        gpre=row(g_pre_mix), gpost_mix=row(g_post_mix), gpre_mlp=row(g_pre_mlp), gpost_mlp=row(g_post_mlp),
        win_ssd=(w_t[:r_ssd] * z_scale).astype(BF16), win_merge=w_t[s_dt:].astype(BF16),
        convw=0.5 * conv_w.astype(F32), convb=0.5 * row(conv_b),
        dtb=jnp.pad(row(dt_bias), ((0, 0), (0, pad))), alog=jnp.pad(row(a_log), ((0, 0), (0, pad))),
        dskip=jnp.repeat(row(d_skip), SSD_HEAD_DIM, axis=1), gssd=row(g_ssd_norm),
        e2=jnp.concatenate([e1, e1], axis=0),
        wssd=w_ssd_out.astype(BF16), wpool=w_pool_group.astype(BF16), pscale=row(pool_scale),
        wo=w_o.astype(BF16), wup=w_up.astype(BF16), wdown=w_down.astype(BF16),
    )


def _history_rows(hist):
    bsz, n, c = hist.shape
    out = jnp.zeros((bsz, n, SUBLANES, c), hist.dtype).at[:, :, SUBLANES - 1, :].set(hist)
    return out.reshape(bsz, n * SUBLANES, c)


def _layer(x, mod, conv_hist, pool_hist, h0, p, *, pos0):
    bsz, seq, d = x.shape
    heads, hdim, nstate = h0.shape[1:]
    h0 = h0.reshape(h0.shape[0], heads * hdim, nstate)
    if bsz == SUBLANES and seq == ROW_GROUPS and pos0 >= POOL_HIST:
        return _layer_packed(x, mod, conv_hist, pool_hist, h0, p, pos0=pos0, state_shape=(heads, hdim, nstate))
    if seq % MIX_TILE == 0:
        lt, lpad = MIX_TILE, seq
    else:
        lt = lpad = -(-seq // SSD_CHUNK) * SSD_CHUNK
    lv = seq - (lpad - lt)
    xp = x if lpad == seq else jnp.pad(x, ((0, 0), (0, lpad - seq), (0, 0)))
    mod = mod[:, :, None, :]
    u, yn, h_new, conv_state = _ssd(xp, mod, _history_rows(conv_hist), h0, p, lt=lt, lv=lv, nseq=1)
    lm = MERGE_TILE if lpad % MERGE_TILE == 0 else lt
    x1, pool_state = _merge(xp, u, yn, mod, _history_rows(pool_hist), p, lt=lm, lv=lv + lm - lt, pos0=pos0, nseq=1)
    tm = MLP_TILE if lpad % MLP_TILE == 0 else lpad
    y = _mlp(x1, mod, p, tm=tm)
    if lpad != seq:
        y = y[:, :seq]
    return y, h_new.reshape(bsz, heads, hdim, nstate), conv_state, pool_state


def _layer_packed(x, mod, conv_hist, pool_hist, h0, p, *, pos0, state_shape):
    bsz, seq, d = x.shape
    rows = bsz * seq
    by_step = lambda a: a.transpose(1, 0, 2).reshape(1, a.shape[1] * bsz, a.shape[2])
    from_step = lambda a, n: a.reshape(n, bsz, a.shape[-1]).transpose(1, 0, 2)
    mod_t = mod.transpose(1, 0, 2)
    mod_tok = jnp.repeat(mod_t, seq, axis=1)[None]
    mod_perm = jnp.tile(mod_t, (1, seq, 1))[None]
    xp = x.reshape(1, rows, d)
    u, yn, h_new, conv_state = _ssd(xp, mod_tok, by_step(conv_hist), h0, p, lt=rows, lv=rows, nseq=bsz)
    x1, pool_state = _merge(xp, u, yn, mod_perm, by_step(pool_hist), p, lt=rows, lv=rows, pos0=pos0, nseq=bsz)
    y = _mlp(x1, mod_tok, p, tm=rows)
    return (y.reshape(bsz, seq, d), h_new.reshape((bsz,) + state_shape),
            from_step(conv_state, CONV_WIDTH - 1), from_step(pool_state, POOL_HIST))


def kernel(x_prompt, x_sample, state_ssm, state_conv, state_pool, c_prompt, c_sample, w_ada, b_ada, g_pre_mix, g_post_mix, g_pre_mlp, g_post_mlp, w_in, conv_w, conv_b, dt_bias, a_log, d_skip, g_ssd_norm, w_ssd_out, w_pool_group, pool_scale, w_o, w_up, w_down):
    yp, ys = x_prompt, x_sample
    bp, d = x_prompt.shape[0], x_prompt.shape[2]
    bs = x_sample.shape[0]
    depth = w_in.shape[0]
    outs = [[] for _ in range(6)]
    for l in range(depth):
        p = _prep_weights(g_pre_mix[l], g_post_mix[l], g_pre_mlp[l], g_post_mlp[l], w_in[l], conv_w[l],
                          conv_b[l], dt_bias[l], a_log[l], d_skip[l], g_ssd_norm[l], w_ssd_out[l],
                          w_pool_group[l], pool_scale[l], w_o[l], w_up[l], w_down[l])
        mod = _adaln(jnp.concatenate([c_prompt, c_sample], axis=0), w_ada[l], b_ada[l])
        mod_p = mod[:bp].reshape(bp, 6, d)
        mod_s = mod[bp:].reshape(bs, 6, d)
        zc = jnp.zeros((1,) + state_conv.shape[2:], yp.dtype)
        zp = jnp.zeros((1,) + state_pool.shape[2:], yp.dtype)
        zh = jnp.zeros((1,) + state_ssm.shape[2:], yp.dtype)
        yp, h_p, c_p, p_p = _layer(yp, mod_p, zc, zp, zh, p, pos0=0)
        ys, h_s, c_s, p_s = _layer(ys, mod_s, state_conv[l], state_pool[l], state_ssm[l], p, pos0=PAST_LEN)
        for lst, val in zip(outs, (h_p, c_p, p_p, h_s, c_s, p_s)):
            lst.append(val)
    return (yp, ys) + tuple(jnp.stack(o) for o in outs)
```

```python
import functools

import jax
import jax.numpy as jnp
from jax import lax
from jax.experimental import pallas as pl
from jax.experimental.pallas import tpu as pltpu

F32 = jnp.float32
BF16 = jnp.bfloat16

EPS = 1e-6
LOG2E = 1.4426950408889634
PAST_LEN = 2048
POOL_WINDOWS = (2, 4, 8, 16)
POOL_HIST = max(POOL_WINDOWS) - 1
CONV_WIDTH = 4
SSD_HEAD_DIM = 64
SSD_STATE = 128
LANES = 128
SUBLANES = 8
SSD_CHUNK = 128
ROW_GROUPS = SSD_CHUNK // SUBLANES
GROUP_COLS = 256
HEADS_PER_GROUP = GROUP_COLS // SSD_HEAD_DIM
PROJ_COLS = 2048
CONV_COLS = SSD_STATE
CONV_CARRY = (CONV_WIDTH - 1) * SUBLANES
POOL_CARRY = POOL_HIST * SUBLANES
MIX_TILE = 512
MERGE_TILE = 1024
MLP_TILE = 1024
VMEM_LIMIT = 60 * 1024 * 1024


def _row_of_token(tok):
    return (tok % ROW_GROUPS) * SUBLANES + tok // ROW_GROUPS


def _tile_row(tok):
    return (tok // SSD_CHUNK) * SSD_CHUNK + _row_of_token(tok % SSD_CHUNK)


def _tokens(shape, axis):
    r = lax.broadcasted_iota(jnp.int32, shape, axis)
    return (r // SSD_CHUNK) * SSD_CHUNK + (r % SUBLANES) * ROW_GROUPS + (r % SSD_CHUNK) // SUBLANES


def _dot(a, b):
    return jnp.dot(a, b, preferred_element_type=F32)


def _dot_wt(a, w_t):
    return lax.dot_general(a, w_t, (((1,), (1,)), ((), ())), preferred_element_type=F32)


def _sigmoid(v):
    return 1.0 / (1.0 + jnp.exp2(v * (-LOG2E)))


def _silu(v):
    return v * _sigmoid(v)


def _silu_of_twice(h):
    return h + h * jnp.tanh(h)


def _rms(v, g):
    return v * lax.rsqrt(jnp.mean(v * v, axis=-1, keepdims=True) + EPS) * g


def _shifted_groups(prev_tail, cur, n_back, packed):
    if packed:
        return [prev_tail[j * SUBLANES:(j + 1) * SUBLANES] for j in range(n_back)]
    first = lax.broadcasted_iota(jnp.int32, (SUBLANES, cur.shape[1]), 0) == 0
    tail = cur[SSD_CHUNK - n_back * SUBLANES:]
    parts = []
    for j in range(n_back):
        rs = slice(j * SUBLANES, (j + 1) * SUBLANES)
        parts.append(jnp.where(first, pltpu.roll(prev_tail[rs], 1, axis=0), pltpu.roll(tail[rs], 1, axis=0)))
    return parts


def _adaln_kernel(c_ref, w_ref, b_ref, o_ref):
    c = c_ref[...]
    o_ref[...] = _dot(_silu(c).astype(BF16), w_ref[...].astype(BF16)) + b_ref[...]


def _adaln(c, w_ada, b_ada):
    rows, d = c.shape
    n = w_ada.shape[1]
    bn = d
    return pl.pallas_call(
        _adaln_kernel,
        out_shape=jax.ShapeDtypeStruct((rows, n), F32),
        grid=(n // bn,),
        in_specs=[pl.BlockSpec((rows, d), lambda j: (0, 0)),
                  pl.BlockSpec((d, bn), lambda j: (0, j)),
                  pl.BlockSpec((1, bn), lambda j: (0, j))],
        out_specs=pl.BlockSpec((rows, bn), lambda j: (0, j)),
        compiler_params=pltpu.CompilerParams(dimension_semantics=("arbitrary",)),
        name="adaln",
    )(c, w_ada, b_ada.reshape(1, n))


def _ssd_kernel(x_ref, mod_ref, convh_ref, h0_ref,
                gpre_ref, win_ref, convw_ref, convb_ref, dtb_ref, alog_ref, dskip_ref,
                gssd_ref, e2_ref,
                u_ref, yn_ref, hout_ref, convo_ref,
                perm_scr, convc_scr, xr_scr, bmt_scr, cm_scr, acs_scr, acst_scr,
                ea_scr, dte_scr, y_scr, st_scr, cb_scr, rcat_scr, mcat_scr,
                *, lt, lv, d_inner, n_groups, nseq):
    q = SSD_CHUNK
    n_chunks = lt // q
    d_bc = n_groups * SSD_STATE
    d_model = x_ref.shape[-1]
    n_slabs = d_model // LANES
    packed = nseq > 1

    def prenorm(dst_ref):
        u = _rms(x_ref[0], gpre_ref[...] * (1.0 + mod_ref[0, 1])) + mod_ref[0, 0]
        for c in range(n_chunks):
            for i in range(ROW_GROUPS):
                start = c * q + (i // 2 if i % 2 == 0 else q // 2 + i // 2)
                r0 = c * q + i * SUBLANES
                for j in range(n_slabs):
                    perm_scr[j, pl.ds(start, SUBLANES, stride=SUBLANES), :] = (
                        u[r0:r0 + SUBLANES, j * LANES:(j + 1) * LANES])
        for j in range(n_slabs):
            dst_ref[:, j * LANES:(j + 1) * LANES] = perm_scr[j].astype(BF16)

    def ssd_tile(t):
        last_t = pl.num_programs(1) - 1

        @pl.when(t == 0)
        def _init():
            convc_scr[...] = convh_ref[0]
            rcat_scr[...] = jnp.zeros(rcat_scr.shape, BF16)
            for s in range(nseq):
                for g in range(n_groups):
                    cs = slice(g * GROUP_COLS, (g + 1) * GROUP_COLS)
                    st_scr[s, :, cs] = h0_ref[s, cs, :].T

        prenorm(u_ref.at[0])

        lane = lax.broadcasted_iota(jnp.int32, (1, LANES), 1)
        n_heads = d_inner // SSD_HEAD_DIM
        r_dt = d_inner + d_inner + 2 * d_bc
        dt = jax.nn.softplus(_dot_wt(u_ref[0], win_ref[r_dt:r_dt + LANES, :]) + dtb_ref[...])
        if lv < lt:
            dt = jnp.where(_tokens((lt, LANES), 0) < lv, dt, 0.0)
        a_row = jnp.where(lane < n_heads, -jnp.exp(alog_ref[...]), 0.0)
        da = dt * a_row
        tok_l, tok_s = _tokens((q, q), 0), _tokens((q, q), 1)
        causal = tok_l >= tok_s
        if packed:
            causal = causal & (tok_l // ROW_GROUPS == tok_s // ROW_GROUPS)
        tri3 = jnp.concatenate([jnp.where(causal, 1.0, 0.0).astype(BF16)] * 3, axis=1)
        ea_parts, dte_parts = [], []
        for c in range(n_chunks):
            da_c = da[c * q:(c + 1) * q]
            hi = da_c.astype(BF16)
            rest = da_c - hi.astype(F32)
            mid = rest.astype(BF16)
            lo = (rest - mid.astype(F32)).astype(BF16)
            acs = _dot(tri3, jnp.concatenate([hi, mid, lo], axis=0))
            acs_scr[c * q:(c + 1) * q, :] = acs * LOG2E
            acst_scr[c] = (acs * LOG2E).T
            ea_parts.append(jnp.exp(acs))
            last = jnp.concatenate([acs[q - SUBLANES:]] * ROW_GROUPS, axis=0) if packed else acs[q - 1:q, :]
            dte_parts.append(jnp.exp(last - acs))

        def split_hi_lo(v):
            hi = v.astype(BF16)
            lo = (v - hi.astype(F32)).astype(BF16)
            return jnp.concatenate([hi, lo], axis=1)

        dt2 = split_hi_lo(dt)
        ea2 = split_hi_lo(jnp.concatenate(ea_parts, axis=0))
        dte2 = split_hi_lo(jnp.concatenate(dte_parts, axis=0))
        for c0 in range(0, d_inner, PROJ_COLS):
            cs = slice(c0, c0 + PROJ_COLS)
            ea_scr[:, cs] = _dot(ea2, e2_ref[:, cs])
            dte_scr[:, cs] = _dot(dte2, e2_ref[:, cs])

        for p0 in range(0, d_inner + 2 * d_bc, PROJ_COLS):
            ps = slice(p0, p0 + PROJ_COLS)
            res_wide = _dot_wt(u_ref[0], win_ref[d_inner + p0:d_inner + p0 + PROJ_COLS, :])
            dt_wide = _dot(dt2, e2_ref[:, ps]) if p0 < d_inner else None
            for h0 in range(0, PROJ_COLS, CONV_COLS):
                c0 = p0 + h0
                cs = slice(c0, c0 + CONV_COLS)
                res = res_wide[:, h0:h0 + CONV_COLS]
                prev_tail = convc_scr[:, cs]
                for c in range(n_chunks):
                    rows = slice(c * q, (c + 1) * q)
                    cur = res[rows]
                    ext = jnp.concatenate(_shifted_groups(prev_tail, cur, CONV_WIDTH - 1, packed) + [cur], axis=0)
                    acc = convb_ref[:, cs] + ext[CONV_CARRY:] * convw_ref[CONV_WIDTH - 1:CONV_WIDTH, cs]
                    for k in range(1, CONV_WIDTH):
                        lo = CONV_CARRY - k * SUBLANES
                        acc = acc + ext[lo:lo + q] * convw_ref[CONV_WIDTH - 1 - k:CONV_WIDTH - k, cs]
                    v = _silu_of_twice(acc)
                    if c0 < d_inner:
                        y_scr[rows, cs] = v * dskip_ref[:, cs]
                        xr_scr[rows, cs] = v * dt_wide[rows, h0:h0 + CONV_COLS]
                    elif c0 < d_inner + d_bc:
                        b0 = c0 - d_inner
                        bmt_scr[c * n_groups + b0 // SSD_STATE] = v.T.astype(BF16)
                    else:
                        b0 = c0 - d_inner - d_bc
                        cm_scr[rows, b0:b0 + CONV_COLS] = v.astype(BF16)
                    prev_tail = cur[q - CONV_CARRY:]
                convc_scr[:, cs] = prev_tail
                if packed:
                    convo_ref[0, :, cs] = res[q - CONV_CARRY:]
                else:
                    for j in range(CONV_WIDTH - 1):
                        r = _tile_row(lv - (CONV_WIDTH - 1) + j)
                        convo_ref[0, j:j + 1, cs] = res[r:r + 1]

        if packed:
            row_seq = lax.broadcasted_iota(jnp.int32, (q, SSD_STATE), 0) % SUBLANES
            col_seq = lax.broadcasted_iota(jnp.int32, (SSD_STATE, q), 1) % SUBLANES
            row_masks = [jnp.where(row_seq == s, 1.0, 0.0).astype(BF16) for s in range(nseq)]
            col_masks = [jnp.where(col_seq == s, 1.0, 0.0).astype(BF16) for s in range(nseq)]

        def chunk_body(c, carry):
            r0 = pl.multiple_of(c * q, q)
            crow = pl.ds(r0, q)
            for g in range(n_groups):
                gs = slice(g * GROUP_COLS, (g + 1) * GROUP_COLS)
                ns = slice(g * SSD_STATE, (g + 1) * SSD_STATE)
                cb_scr[g] = _dot(cm_scr[crow, ns], bmt_scr[c * n_groups + g])
                xr_bf = xr_scr[crow, gs].astype(BF16)
                for j in range(HEADS_PER_GROUP):
                    ls = slice(j * SSD_HEAD_DIM, (j + 1) * SSD_HEAD_DIM)
                    rcat_scr[g, j * q:(j + 1) * q, ls] = xr_bf[:, ls]
            acs = acs_scr[crow, :]
            acst = acst_scr[c]
            for g in range(n_groups):
                cb = cb_scr[g]
                for j in range(HEADS_PER_GROUP):
                    h = g * HEADS_PER_GROUP + j
                    seg = acs[:, h:h + 1] - acst[h:h + 1, :]
                    dec = jnp.exp2(jnp.where(causal, seg, -jnp.inf))
                    mcat_scr[g, :, j * q:(j + 1) * q] = (cb * dec).astype(BF16)
            for g in range(n_groups):
                gs = slice(g * GROUP_COLS, (g + 1) * GROUP_COLS)
                ns = slice(g * SSD_STATE, (g + 1) * SSD_STATE)
                y_diag = _dot(mcat_scr[g], rcat_scr[g])
                xrd = (xr_scr[crow, gs] * dte_scr[crow, gs]).astype(BF16)
                if packed:
                    cg = cm_scr[crow, ns]
                    c_cat = jnp.concatenate([cg * row_masks[s] for s in range(nseq)], axis=1)
                    st_cat = jnp.concatenate([st_scr[s, :, gs].astype(BF16) for s in range(nseq)], axis=0)
                    y_off = _dot(c_cat, st_cat) * ea_scr[crow, gs]
                    bt = bmt_scr[c * n_groups + g]
                    for s in range(nseq):
                        cd = ea_scr[q - SUBLANES + s:q - SUBLANES + s + 1, gs]
                        st_scr[s, :, gs] = st_scr[s, :, gs] * cd + _dot(bt * col_masks[s], xrd)
                else:
                    st = st_scr[0, :, gs]
                    y_off = _dot(cm_scr[crow, ns], st.astype(BF16)) * ea_scr[crow, gs]
                    cd = ea_scr[pl.ds(r0 + q - 1, 1), gs]
                    st_scr[0, :, gs] = st * cd + _dot(bmt_scr[c * n_groups + g], xrd)
                y_scr[crow, gs] = y_scr[crow, gs] + y_diag + y_off
            return carry

        lax.fori_loop(0, n_chunks, chunk_body, 0)

        @pl.when(t == last_t)
        def _ssm_state():
            for s in range(nseq):
                for g in range(n_groups):
                    cs = slice(g * GROUP_COLS, (g + 1) * GROUP_COLS)
                    hout_ref[s, cs, :] = st_scr[s, :, cs].T

        for p0 in range(0, d_inner, PROJ_COLS):
            z_wide = _dot_wt(u_ref[0], win_ref[p0:p0 + PROJ_COLS, :])
            for h0 in range(0, PROJ_COLS, GROUP_COLS):
                gs = slice(p0 + h0, p0 + h0 + GROUP_COLS)
                yn_ref[0, :, gs] = _rms(y_scr[:, gs] * _silu_of_twice(z_wide[:, h0:h0 + GROUP_COLS]),
                                        gssd_ref[:, gs]).astype(BF16)

    ssd_tile(pl.program_id(1))


def _ssd(x, mod, conv_hist, h0, p, *, lt, lv, nseq):
    bsz, lpad, d = x.shape
    nt = lpad // lt
    d_inner = p["dskip"].shape[1]
    conv_dim = p["convw"].shape[1]
    n_groups = (conv_dim - d_inner) // (2 * SSD_STATE)
    n_chunks = lt // SSD_CHUNK
    hp = h0.shape[1]
    mod_rows = mod.shape[2]
    conv_rows = (CONV_WIDTH - 1) * nseq
    assert lv >= CONV_WIDTH - 1 and lt % SSD_CHUNK == 0 and d_inner == n_groups * GROUP_COLS
    assert nseq == 1 or (nseq == SUBLANES and lt == lv == lpad == SSD_CHUNK)

    weights = [p["gpre"], p["win_ssd"], p["convw"], p["convb"], p["dtb"], p["alog"],
               p["dskip"], p["gssd"], p["e2"]]
    tile_map = lambda b, t: (b, t, 0)
    seq_map = lambda b, t: (b, 0, 0)
    shared_map = lambda b, t: (0, 0, 0)
    in_specs = [
        pl.BlockSpec((1, lt, d), tile_map),
        pl.BlockSpec((1, 6, mod_rows, d), lambda b, t: (b, 0, 0, 0)),
        pl.BlockSpec((1, CONV_CARRY, conv_dim), seq_map if conv_hist.shape[0] == bsz else shared_map),
        pl.BlockSpec((nseq, hp, SSD_STATE), seq_map if h0.shape[0] == bsz * nseq else shared_map),
    ] + [pl.BlockSpec(memory_space=pltpu.VMEM) for _ in weights]
    out_shape = (
        jax.ShapeDtypeStruct((bsz, lpad, d), BF16),
        jax.ShapeDtypeStruct((bsz, lpad, d_inner), BF16),
        jax.ShapeDtypeStruct((bsz * nseq, hp, SSD_STATE), F32),
        jax.ShapeDtypeStruct((bsz, conv_rows, conv_dim), F32),
    )
    out_specs = (
        pl.BlockSpec((1, lt, d), tile_map),
        pl.BlockSpec((1, lt, d_inner), tile_map),
        pl.BlockSpec((nseq, hp, SSD_STATE), seq_map),
        pl.BlockSpec((1, conv_rows, conv_dim), seq_map),
    )
    scratch = [
        pltpu.VMEM((d // LANES, lt, LANES), F32),
        pltpu.VMEM((CONV_CARRY, conv_dim), F32),
        pltpu.VMEM((lt, d_inner), F32),
        pltpu.VMEM((n_chunks * n_groups, SSD_STATE, SSD_CHUNK), BF16),
        pltpu.VMEM((lt, n_groups * SSD_STATE), BF16),
        pltpu.VMEM((lt, LANES), F32),
        pltpu.VMEM((n_chunks, LANES, SSD_CHUNK), F32),
        pltpu.VMEM((lt, d_inner), F32),
        pltpu.VMEM((lt, d_inner), F32),
        pltpu.VMEM((lt, d_inner), F32),
        pltpu.VMEM((nseq, SSD_STATE, hp), F32),
        pltpu.VMEM((n_groups, SSD_CHUNK, SSD_CHUNK), F32),
        pltpu.VMEM((n_groups, HEADS_PER_GROUP * SSD_CHUNK, GROUP_COLS), BF16),
        pltpu.VMEM((n_groups, SSD_CHUNK, HEADS_PER_GROUP * SSD_CHUNK), BF16),
    ]
    kern = functools.partial(_ssd_kernel, lt=lt, lv=lv, d_inner=d_inner, n_groups=n_groups, nseq=nseq)
    return pl.pallas_call(
        kern, out_shape=out_shape, grid=(bsz, nt), in_specs=in_specs, out_specs=out_specs,
        scratch_shapes=scratch,
        compiler_params=pltpu.CompilerParams(dimension_semantics=("arbitrary", "arbitrary"),
                                             vmem_limit_bytes=VMEM_LIMIT),
        name="ssd",
    )(x, mod, conv_hist, h0, *weights)


def _merge_kernel(x_ref, u_ref, yn_ref, mod_ref, poolh_ref,
                  gpost_ref, win_ref, wssd_ref, wpool_ref, pscale_ref, wo_ref,
                  out_ref, poolo_ref,
                  perm_scr, poolc_scr, ypool_scr,
                  *, lt, lv, pos0, nseq):
    q = SSD_CHUNK
    n_chunks = lt // q
    t = pl.program_id(1)
    packed = nseq > 1
    d_model = x_ref.shape[-1]
    n_slabs = d_model // LANES

    @pl.when(t == 0)
    def _init():
        poolc_scr[...] = poolh_ref[0]

    d_pool = ypool_scr.shape[1]
    p_all = _dot_wt(u_ref[0], win_ref[0:d_pool, :])
    for gi, w in enumerate(POOL_WINDOWS):
        gs = slice(gi * GROUP_COLS, (gi + 1) * GROUP_COLS)
        res = p_all[:, gs]
        prev_tail = poolc_scr[:, gs]
        for c in range(n_chunks):
            rows = slice(c * q, (c + 1) * q)
            cur = res[rows]
            back = _shifted_groups(prev_tail[POOL_CARRY - (w - 1) * SUBLANES:], cur, w - 1, packed)
            win = jnp.concatenate(back + [cur], axis=0)
            span = 1
            while span < w:
                win = win[span * SUBLANES:] + win[:win.shape[0] - span * SUBLANES]
                span *= 2
            if pos0 >= POOL_HIST or c > 0:
                pm = win * (1.0 / w) - cur
            else:
                cnt = jnp.minimum(_tokens((q, GROUP_COLS), 0) + (pos0 + 1), w).astype(F32)
                inv = jnp.where(t == 0, 1.0 / cnt, 1.0 / w)
                pm = win * inv - cur
            ypool_scr[rows, gs] = _dot(pm.astype(BF16), wpool_ref[gi]) * pscale_ref[:, gs]
            prev_tail = cur[SUBLANES:]
        poolc_scr[:, gs] = prev_tail
        if packed:
            poolo_ref[0, :, gs] = res[q - POOL_CARRY:]
        else:
            for j in range(POOL_HIST):
                r = _tile_row(lv - POOL_HIST + j)
                poolo_ref[0, j:j + 1, gs] = res[r:r + 1]

    y_ssd = _dot(yn_ref[0], wssd_ref[...])
    gate_ssd = _sigmoid(_dot_wt(u_ref[0], win_ref[d_pool:d_pool + d_model, :]))
    gate_pool = _sigmoid(_dot_wt(u_ref[0], win_ref[d_pool + d_model:d_pool + 2 * d_model, :]))
    merged = (gate_ssd * y_ssd + gate_pool * ypool_scr[...]).astype(BF16)
    mix = _dot(merged, wo_ref[...])
    branch = mod_ref[0, 2] * _rms(mix, gpost_ref[...])
    for j in range(n_slabs):
        perm_scr[j] = branch[:, j * LANES:(j + 1) * LANES]
    for c in range(n_chunks):
        for i in range(ROW_GROUPS):
            start = c * q + (i // 2 if i % 2 == 0 else q // 2 + i // 2)
            rows = slice(c * q + i * SUBLANES, c * q + (i + 1) * SUBLANES)
            for j in range(n_slabs):
                ls = slice(j * LANES, (j + 1) * LANES)
                out_ref[0, rows, ls] = x_ref[0, rows, ls] + perm_scr[j, pl.ds(start, SUBLANES, stride=SUBLANES), :]


def _merge(x, u, yn, mod, pool_hist, p, *, lt, lv, pos0, nseq):
    bsz, lpad, d = x.shape
    nt = lpad // lt
    d_inner = yn.shape[-1]
    d_pool = p["pscale"].shape[1]
    mod_rows = mod.shape[2]
    pool_rows = POOL_HIST * nseq
    assert lv >= POOL_HIST and d_pool == len(POOL_WINDOWS) * GROUP_COLS
    assert nseq == 1 or (nseq == SUBLANES and lt == lv == lpad == SSD_CHUNK and pos0 >= POOL_HIST)
    weights = [p["gpost_mix"], p["win_merge"], p["wssd"], p["wpool"], p["pscale"], p["wo"]]
    in_specs = [
        pl.BlockSpec((1, lt, d), lambda b, t: (b, t, 0)),
        pl.BlockSpec((1, lt, d), lambda b, t: (b, t, 0)),
        pl.BlockSpec((1, lt, d_inner), lambda b, t: (b, t, 0)),
        pl.BlockSpec((1, 6, mod_rows, d), lambda b, t: (b, 0, 0, 0)),
        pl.BlockSpec((1, POOL_CARRY, d_pool), lambda b, t: (b if pool_hist.shape[0] == bsz else 0, 0, 0)),
    ] + [pl.BlockSpec(memory_space=pltpu.VMEM) for _ in weights]
    out_shape = (
        jax.ShapeDtypeStruct((bsz, lpad, d), F32),
        jax.ShapeDtypeStruct((bsz, pool_rows, d_pool), F32),
    )
    out_specs = (
        pl.BlockSpec((1, lt, d), lambda b, t: (b, t, 0)),
        pl.BlockSpec((1, pool_rows, d_pool), lambda b, t: (b, 0, 0)),
    )
    scratch = [
        pltpu.VMEM((d // LANES, lt, LANES), F32),
        pltpu.VMEM((POOL_CARRY, d_pool), F32),
        pltpu.VMEM((lt, d_pool), F32),
    ]
    kern = functools.partial(_merge_kernel, lt=lt, lv=lv, pos0=pos0, nseq=nseq)
    return pl.pallas_call(
        kern, out_shape=out_shape, grid=(bsz, nt), in_specs=in_specs, out_specs=out_specs,
        scratch_shapes=scratch,
        compiler_params=pltpu.CompilerParams(dimension_semantics=("arbitrary", "arbitrary"),
                                             vmem_limit_bytes=VMEM_LIMIT),
        name="merge",
    )(x, u, yn, mod, pool_hist, *weights)


def _mlp_kernel(x_ref, mod_ref, gpre_ref, gpost_ref, wup_ref, wdown_ref, out_ref, v_scr):
    x = x_ref[0]
    v_scr[...] = (_rms(x, gpre_ref[...] * (1.0 + mod_ref[0, 4])) + mod_ref[0, 3]).astype(BF16)
    acc = None
    for c0 in range(0, wup_ref.shape[1], PROJ_COLS):
        h = jnp.square(jnp.maximum(_dot(v_scr[...], wup_ref[:, c0:c0 + PROJ_COLS]), 0.0)).astype(BF16)
        part = _dot(h, wdown_ref[c0:c0 + PROJ_COLS, :])
        acc = part if acc is None else acc + part
    out_ref[0] = x + mod_ref[0, 5] * _rms(acc, gpost_ref[...])


def _mlp(x, mod, p, *, tm):
    bsz, lpad, d = x.shape
    nt = lpad // tm
    weights = [p["gpre_mlp"], p["gpost_mlp"], p["wup"], p["wdown"]]
    return pl.pallas_call(
        _mlp_kernel,
        out_shape=jax.ShapeDtypeStruct((bsz, lpad, d), F32),
        grid=(bsz, nt),
        in_specs=[pl.BlockSpec((1, tm, d), lambda b, t: (b, t, 0)),
                  pl.BlockSpec((1, 6, mod.shape[2], d), lambda b, t: (b, 0, 0, 0))]
                 + [pl.BlockSpec(memory_space=pltpu.VMEM) for _ in weights],
        out_specs=pl.BlockSpec((1, tm, d), lambda b, t: (b, t, 0)),
        scratch_shapes=[pltpu.VMEM((tm, d), BF16)],
        compiler_params=pltpu.CompilerParams(dimension_semantics=("arbitrary", "arbitrary"),
                                             vmem_limit_bytes=VMEM_LIMIT,
                                             allow_input_fusion=[False, False, False, False, True, True]),
        name="mlp",
    )(x, mod, *weights)


def _prep_weights(g_pre_mix, g_post_mix, g_pre_mlp, g_post_mlp, w_in, conv_w, conv_b, dt_bias, a_log,
                  d_skip, g_ssd_norm, w_ssd_out, w_pool_group, pool_scale, w_o, w_up, w_down):
    d_inner = w_ssd_out.shape[0]
    conv_dim = conv_w.shape[1]
    n_heads = dt_bias.shape[0]
    d_pool = pool_scale.shape[0]
    s_z, s_xbc = d_inner, d_inner + conv_dim
    s_dt, s_pool = s_xbc + n_heads, s_xbc + n_heads + d_pool
    pad = LANES - n_heads
    head_of_col = jnp.arange(d_inner) // SSD_HEAD_DIM
    e1 = (jnp.arange(LANES)[:, None] == head_of_col[None, :]).astype(BF16)
    row = lambda a: a.reshape(1, -1).astype(F32)
    w_t = jnp.swapaxes(w_in, 0, 1)
    r_ssd = s_xbc + LANES
    z_scale = jnp.where(jnp.arange(r_ssd) < s_z, 0.5, 1.0)[:, None]
    return dict(
        gpre=row(g_pre_mix), gpost_mix=row(g_post_mix), gpre_mlp=row(g_pre_mlp), gpost_mlp=row(g_post_mlp),
        win_ssd=(w_t[:r_ssd] * z_scale).astype(BF16), win_merge=w_t[s_dt:].astype(BF16),
        convw=0.5 * conv_w.astype(F32), convb=0.5 * row(conv_b),
        dtb=jnp.pad(row(dt_bias), ((0, 0), (0, pad))), alog=jnp.pad(row(a_log), ((0, 0), (0, pad))),
        dskip=jnp.repeat(row(d_skip), SSD_HEAD_DIM, axis=1), gssd=row(g_ssd_norm),
        e2=jnp.concatenate([e1, e1], axis=0),
        wssd=w_ssd_out.astype(BF16), wpool=w_pool_group.astype(BF16), pscale=row(pool_scale),
        wo=w_o.astype(BF16), wup=w_up.astype(BF16), wdown=w_down.astype(BF16),
    )


def _history_rows(hist):
    bsz, n, c = hist.shape
    out = jnp.zeros((bsz, n, SUBLANES, c), hist.dtype).at[:, :, SUBLANES - 1, :].set(hist)
    return out.reshape(bsz, n * SUBLANES, c)


def _layer(x, mod, conv_hist, pool_hist, h0, p, *, pos0):
    bsz, seq, d = x.shape
    heads, hdim, nstate = h0.shape[1:]
    h0 = h0.reshape(h0.shape[0], heads * hdim, nstate)
    if bsz == SUBLANES and seq == ROW_GROUPS and pos0 >= POOL_HIST:
        return _layer_packed(x, mod, conv_hist, pool_hist, h0, p, pos0=pos0, state_shape=(heads, hdim, nstate))
    if seq % MIX_TILE == 0:
        lt, lpad = MIX_TILE, seq
    else:
        lt = lpad = -(-seq // SSD_CHUNK) * SSD_CHUNK
    lv = seq - (lpad - lt)
    xp = x if lpad == seq else jnp.pad(x, ((0, 0), (0, lpad - seq), (0, 0)))
    mod = mod[:, :, None, :]
    u, yn, h_new, conv_state = _ssd(xp, mod, _history_rows(conv_hist), h0, p, lt=lt, lv=lv, nseq=1)
    lm = MERGE_TILE if lpad % MERGE_TILE == 0 else lt
    x1, pool_state = _merge(xp, u, yn, mod, _history_rows(pool_hist), p, lt=lm, lv=lv + lm - lt, pos0=pos0, nseq=1)
    tm = MLP_TILE if lpad % MLP_TILE == 0 else lpad
    y = _mlp(x1, mod, p, tm=tm)
    if lpad != seq:
        y = y[:, :seq]
    return y, h_new.reshape(bsz, heads, hdim, nstate), conv_state, pool_state


def _layer_packed(x, mod, conv_hist, pool_hist, h0, p, *, pos0, state_shape):
    bsz, seq, d = x.shape
    rows = bsz * seq
    by_step = lambda a: a.transpose(1, 0, 2).reshape(1, a.shape[1] * bsz, a.shape[2])
    from_step = lambda a, n: a.reshape(n, bsz, a.shape[-1]).transpose(1, 0, 2)
    mod_t = mod.transpose(1, 0, 2)
    mod_tok = jnp.repeat(mod_t, seq, axis=1)[None]
    mod_perm = jnp.tile(mod_t, (1, seq, 1))[None]
    xp = x.reshape(1, rows, d)
    u, yn, h_new, conv_state = _ssd(xp, mod_tok, by_step(conv_hist), h0, p, lt=rows, lv=rows, nseq=bsz)
    x1, pool_state = _merge(xp, u, yn, mod_perm, by_step(pool_hist), p, lt=rows, lv=rows, pos0=pos0, nseq=bsz)
    y = _mlp(x1, mod_tok, p, tm=rows)
    return (y.reshape(bsz, seq, d), h_new.reshape((bsz,) + state_shape),
            from_step(conv_state, CONV_WIDTH - 1), from_step(pool_state, POOL_HIST))


def kernel(x_prompt, x_sample, state_ssm, state_conv, state_pool, c_prompt, c_sample, w_ada, b_ada, g_pre_mix, g_post_mix, g_pre_mlp, g_post_mlp, w_in, conv_w, conv_b, dt_bias, a_log, d_skip, g_ssd_norm, w_ssd_out, w_pool_group, pool_scale, w_o, w_up, w_down):
    yp, ys = x_prompt, x_sample
    bp, d = x_prompt.shape[0], x_prompt.shape[2]
    bs = x_sample.shape[0]
    depth = w_in.shape[0]
    outs = [[] for _ in range(6)]
    for l in range(depth):
        p = _prep_weights(g_pre_mix[l], g_post_mix[l], g_pre_mlp[l], g_post_mlp[l], w_in[l], conv_w[l],
                          conv_b[l], dt_bias[l], a_log[l], d_skip[l], g_ssd_norm[l], w_ssd_out[l],
                          w_pool_group[l], pool_scale[l], w_o[l], w_up[l], w_down[l])
        mod = _adaln(jnp.concatenate([c_prompt, c_sample], axis=0), w_ada[l], b_ada[l])
        mod_p = mod[:bp].reshape(bp, 6, d)
        mod_s = mod[bp:].reshape(bs, 6, d)
        zc = jnp.zeros((1,) + state_conv.shape[2:], yp.dtype)
        zp = jnp.zeros((1,) + state_pool.shape[2:], yp.dtype)
        zh = jnp.zeros((1,) + state_ssm.shape[2:], yp.dtype)
        yp, h_p, c_p, p_p = _layer(yp, mod_p, zc, zp, zh, p, pos0=0)
        ys, h_s, c_s, p_s = _layer(ys, mod_s, state_conv[l], state_pool[l], state_ssm[l], p, pos0=PAST_LEN)
        for lst, val in zip(outs, (h_p, c_p, p_p, h_s, c_s, p_s)):
            lst.append(val)
    return (yp, ys) + tuple(jnp.stack(o) for o in outs)
```
